```python
import numpy as np
import jax
import jax.numpy as jnp
from jax import lax

D_MODEL = 2048
BATCH = 8
SEQ = 2048
DEPTH = 2
DEC_BATCH = 2
DEC_SEQ = 8192
PAST_LEN = 128

MIX_WIDTH = D_MODEL
ATT_WIDTH = MIX_WIDTH // 2
ATT_HEAD_DIM = 128
ATT_HEADS = ATT_WIDTH // ATT_HEAD_DIM
DILATED_BRANCHES = ((128, 1), (512, 4), (2048, 16))
Q_BLOCK = 128
KV_PAD = max(w // 2 for w, _ in DILATED_BRANCHES)
N_REL_BUCKETS = 32
REL_MAX_DISTANCE = 1024
HG_WIDTH = MIX_WIDTH - ATT_WIDTH
HG_KEY_DIM = 128
HG_HEADS = HG_WIDTH // HG_KEY_DIM
HG_VALUE_DIM = HG_WIDTH // HG_HEADS
HG_FORGET_WIDTH = HG_HEADS * HG_KEY_DIM
HG_CHUNK = 64
IN_WIDTHS = (ATT_WIDTH, ATT_WIDTH, ATT_WIDTH, HG_FORGET_WIDTH, HG_FORGET_WIDTH, HG_FORGET_WIDTH, HG_WIDTH, HG_WIDTH)
IN_WIDTH = sum(IN_WIDTHS)
MEM_TOKENS = 256
MEM_HEADS = 4
MEM_HEAD_DIM = D_MODEL // MEM_HEADS
N_EXPERTS = 64
N_GROUPS = 8
EXPERTS_PER_GROUP = N_EXPERTS // N_GROUPS
TOP_GROUPS = 1
TOP_K = 2
D_FF_EXPERT = 1408
MOE_BLOCK = 128
ALPHA = (2 * DEPTH) ** 0.25
BETA = (8 * DEPTH) ** -0.25
LN_EPS = 1e-5
RMS_EPS = 1e-6
NEG_INF = -1e30

kernel_name = 'hymba_style_dilated_hgrn2_moe_encoder'


def _layernorm(x, g, b):
    xf = x.astype(jnp.float32)
    mu = jnp.mean(xf, -1, keepdims=True)
    var = jnp.mean(jnp.square(xf - mu), -1, keepdims=True)
    y = (xf - mu) * lax.rsqrt(var + LN_EPS)
    return (y * g.astype(jnp.float32) + b.astype(jnp.float32)).astype(x.dtype)


def _rel_bucket(rel):
    nb = N_REL_BUCKETS // 2
    ret = (rel > 0).astype(np.int32) * nb
    n = np.abs(rel)
    max_exact = nb // 2
    large = max_exact + (np.log(np.maximum(n, 1) / max_exact) / np.log(REL_MAX_DISTANCE / max_exact)
                         * (nb - max_exact)).astype(np.int32)
    large = np.minimum(large, nb - 1)
    return ret + np.where(n < max_exact, n, large)


def _dilated_attention(q, k, v, rel_bias):
    B, S, H, E = q.shape
    dt = q.dtype
    q = q * (E ** -0.5)
    pad = ((0, 0), (KV_PAD, KV_PAD), (0, 0), (0, 0))
    k_pad = jnp.pad(k, pad)
    v_pad = jnp.pad(v, pad)
    consts = []
    for window, dil in DILATED_BRANCHES:
        half = window // (2 * dil)
        n_a = Q_BLOCK // dil
        n_m = n_a + 2 * half
        off = np.arange(n_m)[None, :] - np.arange(n_a)[:, None] - half
        band = jnp.asarray(np.abs(off) <= half)
        bias = jnp.transpose(rel_bias[_rel_bucket(off * dil)], (2, 0, 1)).astype(jnp.float32)
        consts.append((dil, half, n_a, n_m, band, bias))

    def block(blk):
        s0 = blk * Q_BLOCK
        qb = lax.dynamic_slice_in_dim(q, s0, Q_BLOCK, axis=1)
        outs, lses = [], []
        for dil, half, n_a, n_m, band, bias in consts:
            start = s0 + KV_PAD - half * dil
            kw = lax.dynamic_slice_in_dim(k_pad, start, n_m * dil, axis=1).reshape(B, n_m, dil, H, E)
            vw = lax.dynamic_slice_in_dim(v_pad, start, n_m * dil, axis=1).reshape(B, n_m, dil, H, E)
            q5 = qb.reshape(B, n_a, dil, H, E)
            logits = jnp.einsum('baphe,bmphe->bhpam', q5, kw).astype(jnp.float32) + bias[None, :, None]
            key_pos = s0 - half * dil + jnp.arange(n_m)[None, :] * dil + jnp.arange(dil)[:, None]
            ok = band[None] & ((key_pos >= 0) & (key_pos < S))[:, None, :]
            logits = jnp.where(ok[None, None], logits, NEG_INF)
            mx = jnp.max(logits, -1, keepdims=True)
            p = jnp.exp(logits - mx)
            den = jnp.sum(p, -1, keepdims=True)
            o = jnp.einsum('bhpam,bmphe->bhpae', p, vw.astype(jnp.float32)) / den
            lse = (mx + jnp.log(den))[..., 0]
            outs.append(o.transpose(0, 3, 2, 1, 4).reshape(B, Q_BLOCK, H, E))
            lses.append(lse.transpose(0, 3, 2, 1).reshape(B, Q_BLOCK, H))
        wts = jax.nn.softmax(jnp.stack(lses), axis=0)
        return jnp.sum(wts[..., None] * jnp.stack(outs), axis=0).astype(dt)

    o = lax.map(block, jnp.arange(S // Q_BLOCK))
    return o.transpose(1, 0, 2, 3, 4).reshape(B, S, H * E)


def _layer_lower_bounds(raw):
    p = jax.nn.softmax(raw.astype(jnp.float32), axis=0)
    c = jnp.cumsum(p, axis=0)
    return c - c[:1]


def _forget_gate(z, lb):
    z = z.astype(jnp.float32)
    log_f = jnp.logaddexp(jnp.log(lb), jnp.log1p(-lb) + jax.nn.log_sigmoid(z))
    key = (1.0 - lb) * jax.nn.sigmoid(-z)
    return log_f, key


def _to_chunks(a):
    B, S, H, X = a.shape
    return a.reshape(B, S // HG_CHUNK, HG_CHUNK, H, X).transpose(1, 0, 3, 2, 4)


def _hgrn2_scan(q, k, v, log_f):
    B, S, H, K = q.shape
    V = v.shape[-1]
    tri = jnp.tril(jnp.ones((HG_CHUNK, HG_CHUNK), bool))

    def step(state, inp):
        qc, kc, vc, gc = inp
        b = jnp.cumsum(gc, axis=2)
        o_inter = jnp.einsum('bhtk,bhkv->bhtv', qc * jnp.exp(b), state)
        diff = b[:, :, :, None, :] - b[:, :, None, :, :]
        decay = jnp.exp(jnp.where(tri[:, :, None], diff, -jnp.inf))
        att = jnp.einsum('bhtsk,bhsk->bhts', qc[:, :, :, None, :] * decay, kc)
        o_intra = jnp.einsum('bhts,bhsv->bhtv', att, vc)
        b_last = b[:, :, -1:, :]
        state = (jnp.exp(b_last[:, :, 0, :])[..., None] * state
                 + jnp.einsum('bhsk,bhsv->bhkv', kc * jnp.exp(b_last - b), vc))
        return state, o_intra + o_inter

    init = jnp.zeros((B, H, K, V), jnp.float32)
    _, o = lax.scan(step, init, (_to_chunks(q), _to_chunks(k), _to_chunks(v), _to_chunks(log_f)))
    return o.transpose(1, 0, 3, 2, 4).reshape(B, S, H, V)


def _hgrn2(hq, hf_fwd, hf_bwd, hi, hg, lb_fwd, lb_bwd, norm_w):
    B, S, _ = hq.shape

    def heads(a, width):
        return a.reshape(B, S, HG_HEADS, width)

    q = jax.nn.silu(heads(hq, HG_KEY_DIM).astype(jnp.float32))
    v = heads(hi, HG_VALUE_DIM).astype(jnp.float32)
    log_f_f, k_f = _forget_gate(heads(hf_fwd, HG_KEY_DIM), lb_fwd.reshape(HG_HEADS, HG_KEY_DIM))
    log_f_b, k_b = _forget_gate(heads(hf_bwd, HG_KEY_DIM), lb_bwd.reshape(HG_HEADS, HG_KEY_DIM))
    o_f = _hgrn2_scan(q, k_f, v, log_f_f)
    rev = lambda a: jnp.flip(a, axis=1)
    o_b = rev(_hgrn2_scan(rev(q), rev(k_b), rev(v), rev(log_f_b)))
    o = o_f + o_b
    o = o * lax.rsqrt(jnp.mean(jnp.square(o), -1, keepdims=True) + RMS_EPS) * norm_w.astype(jnp.float32)
    o = o.reshape(B, S, HG_WIDTH) * jax.nn.silu(hg.astype(jnp.float32))
    return o.astype(hq.dtype)


def _mixer(x, w_in, w_out, lb_fwd, lb_bwd, norm_w, rel_bias):
    B, S, _ = x.shape
    splits = [int(c) for c in np.cumsum(IN_WIDTHS)[:-1]]
    aq, ak, av, hq, hff, hfb, hi, hg = jnp.split(x @ w_in, splits, axis=-1)
    shp = (B, S, ATT_HEADS, ATT_HEAD_DIM)
    att = _dilated_attention(aq.reshape(shp), ak.reshape(shp), av.reshape(shp), rel_bias)
    rec = _hgrn2(hq, hff, hfb, hi, hg, lb_fwd, lb_bwd, norm_w)
    return jnp.concatenate([att, rec], axis=-1) @ w_out


def _memory_attention(x, mem, wq, wk, wv, wo):
    B, S, _ = x.shape
    M = mem.shape[1]
    q = (x @ wq).reshape(B, S, MEM_HEADS, MEM_HEAD_DIM)
    k = (mem @ wk).reshape(B, M, MEM_HEADS, MEM_HEAD_DIM)
    v = (mem @ wv).reshape(B, M, MEM_HEADS, MEM_HEAD_DIM)
    logits = jnp.einsum('bshe,bmhe->bhsm', q, k).astype(jnp.float32) * (MEM_HEAD_DIM ** -0.5)
    p = jax.nn.softmax(logits, axis=-1).astype(x.dtype)
    o = jnp.einsum('bhsm,bmhe->bshe', p, v).reshape(B, S, D_MODEL)
    return o @ wo


def _route(xt, router_w, router_bias):
    T = xt.shape[0]
    scores = jax.nn.sigmoid((xt @ router_w).astype(jnp.float32))
    biased = scores + router_bias.astype(jnp.float32)
    grp = biased.reshape(T, N_GROUPS, EXPERTS_PER_GROUP)
    grp_score = jnp.sum(lax.top_k(grp, TOP_K)[0], axis=-1)
    _, gidx = lax.top_k(grp_score, TOP_GROUPS)
    in_group = jnp.any(gidx[:, :, None] == jnp.arange(N_GROUPS)[None, None, :], axis=1)
    masked = jnp.where(in_group[:, :, None], grp, -jnp.inf).reshape(T, N_EXPERTS)
    _, idx = lax.top_k(masked, TOP_K)
    w = jnp.take_along_axis(scores, idx, axis=-1)
    return idx, w / jnp.sum(w, -1, keepdims=True)


def _moe(x, router_w, router_bias, w_gate, w_up, w_down):
    B, S, D = x.shape
    T = B * S
    xt = x.reshape(T, D)
    idx, gw = _route(xt, router_w, router_bias)
    n_assign = T * TOP_K
    flat_e = idx.reshape(-1)
    flat_t = jnp.repeat(jnp.arange(T, dtype=jnp.int32), TOP_K)
    flat_w = gw.reshape(-1).astype(x.dtype)
    order = jnp.argsort(flat_e)
    se, st, sw = flat_e[order], flat_t[order], flat_w[order]
    counts = jnp.bincount(flat_e, length=N_EXPERTS)
    padded = (counts + MOE_BLOCK - 1) // MOE_BLOCK * MOE_BLOCK
    pad_end = jnp.cumsum(padded)
    start = jnp.cumsum(counts) - counts
    dest = (pad_end - padded)[se] + jnp.arange(n_assign) - start[se]
    n_blocks = -(-n_assign // MOE_BLOCK) + N_EXPERTS
    n_rows = n_blocks * MOE_BLOCK
    row_tok = jnp.zeros((n_rows,), jnp.int32).at[dest].set(st)
    row_w = jnp.zeros((n_rows,), x.dtype).at[dest].set(sw)
    block_expert = jnp.minimum(
        jnp.searchsorted(pad_end, jnp.arange(n_blocks) * MOE_BLOCK, side='right'), N_EXPERTS - 1)

    def block(args):
        tok, wt, e = args
        xb = jnp.take(xt, tok, axis=0)
        h = jax.nn.silu(xb @ w_gate[e]) * (xb @ w_up[e])
        return (h @ w_down[e]) * wt[:, None]

    y_rows = lax.map(block, (row_tok.reshape(n_blocks, MOE_BLOCK), row_w.reshape(n_blocks, MOE_BLOCK), block_expert))
    y = jnp.zeros_like(xt).at[row_tok].add(y_rows.reshape(n_rows, D))
    return y.reshape(B, S, D)


def _trunk(x, mem, w_in, w_out, lb_fwd, lb_bwd, hg_norm_w, rel_bias, wq_c, wk_c, wv_c, wo_c,
           router_w, router_bias, w_gate, w_up, w_down, ln1_g, ln1_b, ln2_g, ln2_b, ln3_g, ln3_b):
    lbf = _layer_lower_bounds(lb_fwd)
    lbb = _layer_lower_bounds(lb_bwd)
    for l in range(DEPTH):
        h = _mixer(x, w_in[l], w_out[l], lbf[l], lbb[l], hg_norm_w[l], rel_bias)
        x = _layernorm(ALPHA * x + h, ln1_g[l], ln1_b[l])
        h = _memory_attention(x, mem, wq_c[l], wk_c[l], wv_c[l], wo_c[l])
        x = _layernorm(ALPHA * x + h, ln2_g[l], ln2_b[l])
        h = _moe(x, router_w, router_bias, w_gate[l], w_up[l], w_down[l])
        x = _layernorm(ALPHA * x + h, ln3_g[l], ln3_b[l])
    return x


def setup_inputs(seed: int = 0) -> dict:
    key = jax.random.key(seed)
    ks = jax.random.split(key, 26)
    f32 = jnp.float32

    def nrm(k, shape, scale=1.0):
        return jax.random.normal(k, shape, f32) * scale

    return {
        'x_prompt': nrm(ks[0], (BATCH, SEQ, D_MODEL)),
        'x_sample': nrm(ks[1], (DEC_BATCH, DEC_SEQ, D_MODEL)),
        'mem_prompt': nrm(ks[2], (BATCH, MEM_TOKENS, D_MODEL)),
        'mem_sample': nrm(ks[3], (DEC_BATCH, MEM_TOKENS, D_MODEL)),
        'w_in': nrm(ks[4], (DEPTH, D_MODEL, IN_WIDTH), D_MODEL ** -0.5),
        'w_out': nrm(ks[5], (DEPTH, MIX_WIDTH, D_MODEL), BETA * MIX_WIDTH ** -0.5),
        'lb_fwd': nrm(ks[6], (DEPTH, HG_FORGET_WIDTH)),
        'lb_bwd': nrm(ks[7], (DEPTH, HG_FORGET_WIDTH)),
        'hg_norm_w': 1.0 + nrm(ks[8], (DEPTH, HG_VALUE_DIM), 0.02),
        'rel_bias': nrm(ks[9], (N_REL_BUCKETS, ATT_HEADS), 0.5),
        'wq_c': nrm(ks[10], (DEPTH, D_MODEL, D_MODEL), D_MODEL ** -0.5),
        'wk_c': nrm(ks[11], (DEPTH, D_MODEL, D_MODEL), D_MODEL ** -0.5),
        'wv_c': nrm(ks[12], (DEPTH, D_MODEL, D_MODEL), D_MODEL ** -0.5),
        'wo_c': nrm(ks[13], (DEPTH, D_MODEL, D_MODEL), BETA * D_MODEL ** -0.5),
        'router_w': nrm(ks[14], (D_MODEL, N_EXPERTS), D_MODEL ** -0.5),
        'router_bias': nrm(ks[15], (N_EXPERTS,), 0.01),
        'w_gate': nrm(ks[16], (DEPTH, N_EXPERTS, D_MODEL, D_FF_EXPERT), D_MODEL ** -0.5),
        'w_up': nrm(ks[17], (DEPTH, N_EXPERTS, D_MODEL, D_FF_EXPERT), D_MODEL ** -0.5),
        'w_down': nrm(ks[18], (DEPTH, N_EXPERTS, D_FF_EXPERT, D_MODEL), BETA * D_FF_EXPERT ** -0.5),
        'ln1_g': 1.0 + nrm(ks[19], (DEPTH, D_MODEL), 0.02),
        'ln1_b': nrm(ks[20], (DEPTH, D_MODEL), 0.02),
        'ln2_g': 1.0 + nrm(ks[21], (DEPTH, D_MODEL), 0.02),
        'ln2_b': nrm(ks[22], (DEPTH, D_MODEL), 0.02),
        'ln3_g': 1.0 + nrm(ks[23], (DEPTH, D_MODEL), 0.02),
        'ln3_b': nrm(ks[24], (DEPTH, D_MODEL), 0.02),
    }


def reference(x_prompt, x_sample, mem_prompt, mem_sample, w_in, w_out, lb_fwd, lb_bwd, hg_norm_w, rel_bias,
              wq_c, wk_c, wv_c, wo_c, router_w, router_bias, w_gate, w_up, w_down,
              ln1_g, ln1_b, ln2_g, ln2_b, ln3_g, ln3_b):
    weights = (w_in, w_out, lb_fwd, lb_bwd, hg_norm_w, rel_bias, wq_c, wk_c, wv_c, wo_c,
               router_w, router_bias, w_gate, w_up, w_down, ln1_g, ln1_b, ln2_g, ln2_b, ln3_g, ln3_b)
    y_prompt = _trunk(x_prompt, mem_prompt, *weights)
    y_sample = _trunk(x_sample, mem_sample, *weights)
    return (y_prompt, y_sample)
```

```python
import functools

import numpy as np
import jax
import jax.numpy as jnp
from jax import lax
from jax.experimental import pallas as pl
from jax.experimental.pallas import tpu as pltpu

F32 = jnp.float32
BF16 = jnp.bfloat16

D_MODEL = 2048
DEPTH = 2
ATT_WIDTH = 1024
ATT_HEAD_DIM = 128
ATT_HEADS = 8
DILATIONS = (1, 4, 16)
ATT_HALF = 64
N_REL_BUCKETS = 32
REL_MAX_DISTANCE = 1024
HG_WIDTH = 1024
HG_HEADS = 8
HG_DIM = 128
HG_CHUNK = 64
MEM_TOKENS = 256
MEM_HEADS = 4
MEM_HEAD_DIM = 512
N_EXPERTS = 64
N_GROUPS = 8
EXPERTS_PER_GROUP = 8
D_FF = 1408
ALPHA = (2 * DEPTH) ** 0.25
LN_EPS = 1e-5
RMS_EPS = 1e-6
NEG_INF = -1e30

LANES = 128
ATT_BLOCK = 2048
ATT_PAD = 1024
ATT_QC = 128
ATT_KC = ATT_QC + 2 * ATT_HALF
HG_BLOCK = 512
MOE_BLOCK = 256
VMEM_LIMIT = 56 * 1024 * 1024

COL_AQ, COL_AK, COL_AV, COL_HQ, COL_HFF, COL_HFB, COL_HI, COL_HG = (0, 8, 16, 24, 32, 40, 48, 56)


def _cparams(sem):
    return pltpu.CompilerParams(dimension_semantics=sem, vmem_limit_bytes=VMEM_LIMIT)


def _seq_of_block(g, blk, groups):
    (n0, s0), (n1, s1) = groups
    nb0 = n0 * s0 // blk
    per0, per1 = s0 // blk, s1 // blk
    in0 = g < nb0
    first = jnp.where(in0, (g // per0) * per0, nb0 + ((g - nb0) // per1) * per1)
    last = first + jnp.where(in0, per0, per1) - 1
    slen = jnp.where(in0, s0, s1)
    return first, last, slen


def _mm_kernel(x_ref, w_ref, o_ref):
    o_ref[...] = jnp.dot(x_ref[...].astype(BF16), w_ref[...],
                         preferred_element_type=F32).astype(o_ref.dtype)


def _matmul(x, w, tm, tn, out_dtype):
    t, k = x.shape
    n = w.shape[1]
    return pl.pallas_call(
        _mm_kernel,
        grid=(t // tm, n // tn),
        in_specs=[pl.BlockSpec((tm, k), lambda i, j: (i, 0)),
                  pl.BlockSpec((k, tn), lambda i, j: (0, j))],
        out_specs=pl.BlockSpec((tm, tn), lambda i, j: (i, j)),
        out_shape=jax.ShapeDtypeStruct((t, n), out_dtype),
        compiler_params=_cparams(("parallel", "parallel")),
        name="dense_matmul",
    )(x, w)


def _rel_bucket(rel):
    nb = N_REL_BUCKETS // 2
    ret = (rel > 0).astype(np.int32) * nb
    n = np.abs(rel)
    max_exact = nb // 2
    large = max_exact + (np.log(np.maximum(n, 1) / max_exact) / np.log(REL_MAX_DISTANCE / max_exact)
                         * (nb - max_exact)).astype(np.int32)
    large = np.minimum(large, nb - 1)
    return ret + np.where(n < max_exact, n, large)


def _att_bias_table(rel_bias):
    off = np.arange(ATT_KC)[None, :] - np.arange(ATT_QC)[:, None] - ATT_HALF
    band = np.abs(off) <= ATT_HALF
    tabs = []
    for dil in DILATIONS:
        b = jnp.transpose(rel_bias[_rel_bucket(off * dil)], (2, 0, 1)).astype(F32)
        tabs.append(jnp.where(jnp.asarray(band)[None], b, NEG_INF))
    return jnp.stack(tabs)


def _att_kernel(q_ref, kp_ref, kc_ref, kn_ref, vp_ref, vc_ref, vn_ref, bias_ref, o_ref,
                kwin, vwin, acc_s, m_s, l_s, *, groups):
    g = pl.program_id(0)
    first, _, slen = _seq_of_block(g, ATT_BLOCK, groups)
    pos0 = (g - first) * ATT_BLOCK

    kwin[0:ATT_PAD, :] = kp_ref[ATT_BLOCK - ATT_PAD:, :]
    kwin[ATT_PAD:ATT_PAD + ATT_BLOCK, :] = kc_ref[...]
    kwin[ATT_PAD + ATT_BLOCK:, :] = kn_ref[0:ATT_PAD, :]
    vwin[0:ATT_PAD, :] = vp_ref[ATT_BLOCK - ATT_PAD:, :]
    vwin[ATT_PAD:ATT_PAD + ATT_BLOCK, :] = vc_ref[...]
    vwin[ATT_PAD + ATT_BLOCK:, :] = vn_ref[0:ATT_PAD, :]

    scale = ATT_HEAD_DIM ** -0.5
    col = lax.broadcasted_iota(jnp.int32, (1, ATT_KC), 1)
    n_chunks = ATT_BLOCK // ATT_QC
    for bi, dil in enumerate(DILATIONS):
        bias = bias_ref[bi, 0]
        for c in range(n_chunks):
            grp, ph = c // dil, c % dil
            qs = ATT_QC * dil * grp + ph
            ks = ATT_PAD + qs - ATT_HALF * dil
            rows_q = pl.ds(qs, ATT_QC, stride=dil) if dil > 1 else pl.ds(qs, ATT_QC)
            rows_k = pl.ds(ks, ATT_KC, stride=dil) if dil > 1 else pl.ds(ks, ATT_KC)
            qc = (q_ref[rows_q, :] * scale).astype(BF16)
            kc = kwin[rows_k, :].astype(BF16)
            vc = vwin[rows_k, :].astype(BF16)
            s = lax.dot_general(qc, kc, (((1,), (1,)), ((), ())), preferred_element_type=F32)
            s = s + bias
            kpos = pos0 + (ks - ATT_PAD) + dil * col
            s = jnp.where((kpos >= 0) & (kpos < slen), s, NEG_INF)
            m = jnp.max(s, axis=-1, keepdims=True)
            p = jnp.exp(s - m)
            l = jnp.sum(p, axis=-1, keepdims=True)
            o = jnp.dot(p.astype(BF16), vc, preferred_element_type=F32)
            acc_s[bi, rows_q, :] = o
            m_s[bi, rows_q, :] = jnp.broadcast_to(m, (ATT_QC, LANES))
            l_s[bi, rows_q, :] = jnp.broadcast_to(l, (ATT_QC, LANES))

    m = jnp.maximum(jnp.maximum(m_s[0], m_s[1]), m_s[2])
    num = jnp.zeros((ATT_BLOCK, LANES), F32)
    den = jnp.zeros((ATT_BLOCK, LANES), F32)
    for bi in range(len(DILATIONS)):
        w = jnp.exp(m_s[bi] - m)
        num = num + w * acc_s[bi]
        den = den + w * l_s[bi]
    o_ref[...] = num / den


def _dilated_attention(proj, bias_tab, groups):
    t = proj.shape[0]
    nblk = t // ATT_BLOCK
    blk = (ATT_BLOCK, LANES)

    def prev_map(col):
        def f(g, h):
            first, _, _ = _seq_of_block(g, ATT_BLOCK, groups)
            return (jnp.maximum(g - 1, first), col + h)
        return f

    def next_map(col):
        def f(g, h):
            _, last, _ = _seq_of_block(g, ATT_BLOCK, groups)
            return (jnp.minimum(g + 1, last), col + h)
        return f

    def cur_map(col):
        return lambda g, h: (g, col + h)

    n_br = len(DILATIONS)
    return pl.pallas_call(
        functools.partial(_att_kernel, groups=groups),
        grid=(nblk, ATT_HEADS),
        in_specs=[pl.BlockSpec(blk, cur_map(COL_AQ)),
                  pl.BlockSpec(blk, prev_map(COL_AK)), pl.BlockSpec(blk, cur_map(COL_AK)),
                  pl.BlockSpec(blk, next_map(COL_AK)),
                  pl.BlockSpec(blk, prev_map(COL_AV)), pl.BlockSpec(blk, cur_map(COL_AV)),
                  pl.BlockSpec(blk, next_map(COL_AV)),
                  pl.BlockSpec((n_br, 1, ATT_QC, ATT_KC), lambda g, h: (0, h, 0, 0))],
        out_specs=pl.BlockSpec(blk, lambda g, h: (g, h)),
        out_shape=jax.ShapeDtypeStruct((t, ATT_WIDTH), F32),
        scratch_shapes=[pltpu.VMEM((ATT_BLOCK + 2 * ATT_PAD, LANES), F32),
                        pltpu.VMEM((ATT_BLOCK + 2 * ATT_PAD, LANES), F32),
                        pltpu.VMEM((n_br, ATT_BLOCK, LANES), F32),
                        pltpu.VMEM((n_br, ATT_BLOCK, LANES), F32),
                        pltpu.VMEM((n_br, ATT_BLOCK, LANES), F32)],
        compiler_params=_cparams(("parallel", "parallel")),
        name="dilated_attention",
    )(proj, proj, proj, proj, proj, proj, proj, bias_tab)


HG_LEVELS = (32, 16, 8, 4, 2, 1)
HG_XROWS = HG_CHUNK * (2 + len(HG_LEVELS))


def _hg_constants(backward):
    c = HG_CHUNK
    t = np.arange(c)
    if backward:
        t = c - 1 - t
    tt, uu = t[:, None], t[None, :]
    mats = [(uu <= tt), (uu > tt)]
    masks, roles = [], []
    for h in HG_LEVELS:
        same = (tt // h) == (uu // h)
        upper = ((t // h) % 2 == 1)
        seg = np.where(upper[:, None], same & (uu <= tt), same & (uu > tt))
        mats.append(seg)
        pair = ((tt // (2 * h)) == (uu // (2 * h))) & upper[:, None] & (~upper)[None, :]
        masks.append(pair)
        roles.append(upper[:, None])
    masks.append(tt == uu)
    m = np.concatenate(mats, axis=0).astype(np.float32)
    m3 = np.concatenate([m, m, m], axis=1)
    return (jnp.asarray(m3, BF16), jnp.asarray(np.stack(masks), F32),
            jnp.asarray(np.stack(roles), F32))


def _hg_chunk(z, q, v, par, seg_ref, mask_ref, role_ref, st_ref, last_row):
    c = HG_CHUNK
    log_lb, log1m_lb, om_lb = par
    e = jnp.exp(-jnp.abs(z))
    log_sig = jnp.minimum(z, 0.0) - jnp.log1p(e)
    cc = log1m_lb + log_sig
    delta = jnp.minimum(jnp.abs(log_lb - cc), -NEG_INF)
    gl = jnp.maximum(log_lb, cc) + jnp.log1p(jnp.exp(-delta))
    k = om_lb * (jnp.where(z >= 0, e, 1.0) / (1.0 + e))

    g1 = gl.astype(BF16)
    r1 = gl - g1.astype(F32)
    g2 = r1.astype(BF16)
    g3 = (r1 - g2.astype(F32)).astype(BF16)
    sums = jnp.dot(seg_ref[...], jnp.concatenate([g1, g2, g3], axis=0),
                   preferred_element_type=F32)
    dec = jnp.exp(sums)

    st = st_ref[...]
    nt = (((1,), (1,)), ((), ()))
    o = lax.dot_general((q * dec[0:c]).astype(BF16), st.astype(BF16), nt,
                        preferred_element_type=F32)
    att = lax.dot_general(q.astype(BF16), k.astype(BF16), nt,
                          preferred_element_type=F32) * mask_ref[len(HG_LEVELS)]
    for li in range(len(HG_LEVELS)):
        y = (jnp.where(role_ref[li] > 0, q, k) * dec[(2 + li) * c:(3 + li) * c]).astype(BF16)
        att = att + lax.dot_general(y, y, nt, preferred_element_type=F32) * mask_ref[li]
    vb = v.astype(BF16)
    o = o + jnp.dot(att.astype(BF16), vb, preferred_element_type=F32)
    kd = (k * dec[c:2 * c]).astype(BF16)
    upd = lax.dot_general(vb, kd, (((0,), (0,)), ((), ())), preferred_element_type=F32)
    st_ref[...] = st * dec[last_row:last_row + 1] + upd
    return o


def _hgrn_kernel(qf_ref, zf_ref, vf_ref, qb_ref, zb_ref, vb_ref, par_ref,
                 segf_ref, maskf_ref, rolef_ref, segb_ref, maskb_ref, roleb_ref,
                 of_ref, ob_ref, stf_ref, stb_ref, *, groups, nblk):
    i = pl.program_id(1)
    j = nblk - 1 - i
    first_f, _, _ = _seq_of_block(i, HG_BLOCK, groups)
    _, last_b, _ = _seq_of_block(j, HG_BLOCK, groups)

    @pl.when(i == first_f)
    def _():
        stf_ref[...] = jnp.zeros_like(stf_ref)

    @pl.when(j == last_b)
    def _():
        stb_ref[...] = jnp.zeros_like(stb_ref)

    par = par_ref[0]
    par_f = (par[0:1], par[1:2], par[2:3])
    par_b = (par[3:4], par[4:5], par[5:6])
    n_chunks = HG_BLOCK // HG_CHUNK

    def body(ci, carry):
        rf = pl.ds(pl.multiple_of(ci * HG_CHUNK, HG_CHUNK), HG_CHUNK)
        rb = pl.ds(pl.multiple_of((n_chunks - 1 - ci) * HG_CHUNK, HG_CHUNK), HG_CHUNK)
        qf = qf_ref[rf, :]
        qb = qb_ref[rb, :]
        of_ref[rf, :] = _hg_chunk(zf_ref[rf, :], qf * jax.nn.sigmoid(qf), vf_ref[rf, :], par_f,
                                  segf_ref, maskf_ref, rolef_ref, stf_ref, HG_CHUNK - 1)
        ob_ref[rb, :] = _hg_chunk(zb_ref[rb, :], qb * jax.nn.sigmoid(qb), vb_ref[rb, :], par_b,
                                  segb_ref, maskb_ref, roleb_ref, stb_ref, 0)
        return carry

    lax.fori_loop(0, n_chunks, body, 0)


def _hgrn_scan(proj, gate_par, groups):
    t = proj.shape[0]
    nblk = t // HG_BLOCK
    blk = (HG_BLOCK, LANES)
    consts_f = _hg_constants(False)
    consts_b = _hg_constants(True)

    def fmap(col):
        return lambda h, i: (i, col + h)

    def bmap(col):
        return lambda h, i: (nblk - 1 - i, col + h)

    def const_spec(a):
        return pl.BlockSpec(a.shape, lambda h, i, nd=a.ndim: (0,) * nd)

    out = jax.ShapeDtypeStruct((t, HG_WIDTH), F32)
    return pl.pallas_call(
        functools.partial(_hgrn_kernel, groups=groups, nblk=nblk),
        grid=(HG_HEADS, nblk),
        in_specs=[pl.BlockSpec(blk, fmap(COL_HQ)), pl.BlockSpec(blk, fmap(COL_HFF)),
                  pl.BlockSpec(blk, fmap(COL_HI)),
                  pl.BlockSpec(blk, bmap(COL_HQ)), pl.BlockSpec(blk, bmap(COL_HFB)),
                  pl.BlockSpec(blk, bmap(COL_HI)),
                  pl.BlockSpec((1, 8, LANES), lambda h, i: (h, 0, 0))]
                 + [const_spec(a) for a in consts_f] + [const_spec(a) for a in consts_b],
        out_specs=[pl.BlockSpec(blk, lambda h, i: (i, h)),
                   pl.BlockSpec(blk, lambda h, i: (nblk - 1 - i, h))],
        out_shape=[out, out],
        scratch_shapes=[pltpu.VMEM((HG_DIM, HG_DIM), F32), pltpu.VMEM((HG_DIM, HG_DIM), F32)],
        compiler_params=_cparams(("parallel", "arbitrary")),
        name="hgrn2_scan",
    )(proj, proj, proj, proj, proj, proj, gate_par, *consts_f, *consts_b)


def _gate_params(lb_f, lb_b):
    rows = []
    for lb in (lb_f, lb_b):
        lb = lb.reshape(HG_HEADS, 1, HG_DIM)
        rows += [jnp.log(lb), jnp.log1p(-lb), 1.0 - lb]
    rows += [jnp.zeros_like(rows[0])] * 2
    return jnp.concatenate(rows, axis=1)


def _layer_lower_bounds(raw):
    p = jax.nn.softmax(raw.astype(F32), axis=0)
    c = jnp.cumsum(p, axis=0)
    return c - c[:1]


def _layernorm_rows(y, g, b):
    mu = jnp.mean(y, axis=-1, keepdims=True)
    d = y - mu
    var = jnp.mean(d * d, axis=-1, keepdims=True)
    return d * lax.rsqrt(var + LN_EPS) * g + b


def _mix_out_kernel(att_ref, of_ref, ob_ref, hg_ref, x_ref, w_ref, nw_ref, g_ref, b_ref, o_ref):
    o = of_ref[...] + ob_ref[...]
    nw = nw_ref[...]
    parts = []
    for h in range(HG_HEADS):
        oh = o[:, h * HG_DIM:(h + 1) * HG_DIM]
        parts.append(oh * lax.rsqrt(jnp.mean(oh * oh, axis=-1, keepdims=True) + RMS_EPS) * nw)
    hg = hg_ref[...]
    rec = jnp.concatenate(parts, axis=-1) * (hg * jax.nn.sigmoid(hg))
    h = jnp.dot(att_ref[...].astype(BF16), w_ref[0:ATT_WIDTH, :], preferred_element_type=F32)
    h = h + jnp.dot(rec.astype(BF16), w_ref[ATT_WIDTH:, :], preferred_element_type=F32)
    o_ref[...] = _layernorm_rows(ALPHA * x_ref[...] + h, g_ref[...], b_ref[...])


def _mix_out(att, o_f, o_b, proj, x, w_out, norm_w, ln_g, ln_b, tm):
    t = x.shape[0]
    half = pl.BlockSpec((tm, HG_WIDTH), lambda i: (i, 0))
    full = pl.BlockSpec((tm, D_MODEL), lambda i: (i, 0))
    vec = pl.BlockSpec((1, D_MODEL), lambda i: (0, 0))
    return pl.pallas_call(
        _mix_out_kernel,
        grid=(t // tm,),
        in_specs=[half, half, half,
                  pl.BlockSpec((tm, HG_WIDTH), lambda i: (i, COL_HG * LANES // HG_WIDTH)),
                  full,
                  pl.BlockSpec((D_MODEL, D_MODEL), lambda i: (0, 0), pipeline_mode=pl.Buffered(1)),
                  pl.BlockSpec((1, HG_DIM), lambda i: (0, 0)), vec, vec],
        out_specs=full,
        out_shape=jax.ShapeDtypeStruct((t, D_MODEL), F32),
        compiler_params=_cparams(("parallel",)),
        name="mixer_out_ln",
    )(att, o_f, o_b, proj, x, w_out, norm_w, ln_g, ln_b)


def _mem_attn_kernel(x_ref, wq_ref, kv_ref, wo_ref, g_ref, b_ref, o_ref):
    x = x_ref[...]
    q = jnp.dot(x.astype(BF16), wq_ref[...], preferred_element_type=F32)
    scale = MEM_HEAD_DIM ** -0.5
    h_out = jnp.zeros(x.shape, F32)
    for h in range(MEM_HEADS):
        lo, hi = h * MEM_HEAD_DIM, (h + 1) * MEM_HEAD_DIM
        kh = kv_ref[:, lo:hi]
        vh = kv_ref[:, D_MODEL + lo:D_MODEL + hi]
        s = lax.dot_general(q[:, lo:hi].astype(BF16), kh, (((1,), (1,)), ((), ())),
                            preferred_element_type=F32) * scale
        e = jnp.exp(s - jnp.max(s, axis=-1, keepdims=True))
        p = e / jnp.sum(e, axis=-1, keepdims=True)
        oh = jnp.dot(p.astype(BF16), vh, preferred_element_type=F32)
        h_out = h_out + jnp.dot(oh.astype(BF16), wo_ref[lo:hi, :], preferred_element_type=F32)
    o_ref[...] = _layernorm_rows(ALPHA * x + h_out, g_ref[...], b_ref[...])


def _mem_attention(x, kv, wq, wo, ln_g, ln_b, groups, tm):
    t = x.shape[0]
    (n0, s0), (n1, s1) = groups

    def kv_map(i):
        row = i * tm
        t0 = n0 * s0
        return (jnp.where(row < t0, row // s0, n0 + (row - t0) // s1), 0)

    full = pl.BlockSpec((tm, D_MODEL), lambda i: (i, 0))
    vec = pl.BlockSpec((1, D_MODEL), lambda i: (0, 0))
    wspec = pl.BlockSpec((D_MODEL, D_MODEL), lambda i: (0, 0), pipeline_mode=pl.Buffered(1))
    return pl.pallas_call(
        _mem_attn_kernel,
        grid=(t // tm,),
        in_specs=[full, wspec, pl.BlockSpec((MEM_TOKENS, 2 * D_MODEL), kv_map), wspec, vec, vec],
        out_specs=full,
        out_shape=jax.ShapeDtypeStruct((t, D_MODEL), F32),
        compiler_params=_cparams(("parallel",)),
        name="memory_attention_ln",
    )(x, wq, kv, wo, ln_g, ln_b)


def _router_kernel(x_ref, whi_ref, wlo_ref, bias_ref, idx_ref, gw_ref):
    x = x_ref[...]
    x_hi = x.astype(BF16)
    x_lo = (x - x_hi.astype(F32)).astype(BF16)
    nt = (((1,), (1,)), ((), ()))
    whi = whi_ref[...]
    logits = (lax.dot_general(whi, x_hi, nt, preferred_element_type=F32)
              + lax.dot_general(whi, x_lo, nt, preferred_element_type=F32)
              + lax.dot_general(wlo_ref[...], x_hi, nt, preferred_element_type=F32))
    tm = logits.shape[1]
    scores = 1.0 / (1.0 + jnp.exp(-logits))
    biased = scores + bias_ref[...]
    g3 = biased.reshape(N_GROUPS, EXPERTS_PER_GROUP, tm)
    sub = lax.broadcasted_iota(jnp.int32, g3.shape, 1).astype(F32)
    m1 = jnp.max(g3, axis=1, keepdims=True)
    a1 = jnp.min(jnp.where(g3 == m1, sub, float(EXPERTS_PER_GROUP)), axis=1, keepdims=True)
    m2 = jnp.max(jnp.where(sub == a1, -jnp.inf, g3), axis=1, keepdims=True)
    gscore = (m1 + m2).reshape(N_GROUPS, tm)
    gi = lax.broadcasted_iota(jnp.int32, gscore.shape, 0).astype(F32)
    gmax = jnp.max(gscore, axis=0, keepdims=True)
    gsel = jnp.min(jnp.where(gscore == gmax, gi, float(N_GROUPS)), axis=0, keepdims=True)
    ei_int = lax.broadcasted_iota(jnp.int32, biased.shape, 0)
    ei = ei_int.astype(F32)
    egrp = (ei_int // EXPERTS_PER_GROUP).astype(F32)
    masked = jnp.where(egrp == gsel, biased, -jnp.inf)
    v1 = jnp.max(masked, axis=0, keepdims=True)
    i1 = jnp.min(jnp.where(masked == v1, ei, float(N_EXPERTS)), axis=0, keepdims=True)
    masked2 = jnp.where(ei == i1, -jnp.inf, masked)
    v2 = jnp.max(masked2, axis=0, keepdims=True)
    i2 = jnp.min(jnp.where(masked2 == v2, ei, float(N_EXPERTS)), axis=0, keepdims=True)
    w1 = jnp.sum(jnp.where(ei == i1, scores, 0.0), axis=0, keepdims=True)
    w2 = jnp.sum(jnp.where(ei == i2, scores, 0.0), axis=0, keepdims=True)
    tot = w1 + w2
    idx_ref[...] = jnp.concatenate([i1, i2], axis=0).astype(jnp.int32)
    gw_ref[...] = jnp.concatenate([w1 / tot, w2 / tot], axis=0)


def _route(x, w_hi, w_lo, bias_col, tm):
    t = x.shape[0]
    wspec = pl.BlockSpec((N_EXPERTS, D_MODEL), lambda i: (0, 0))
    return pl.pallas_call(
        _router_kernel,
        grid=(t // tm,),
        in_specs=[pl.BlockSpec((tm, D_MODEL), lambda i: (i, 0)), wspec, wspec,
                  pl.BlockSpec((N_EXPERTS, 1), lambda i: (0, 0))],
        out_specs=[pl.BlockSpec((2, tm), lambda i: (0, i)), pl.BlockSpec((2, tm), lambda i: (0, i))],
        out_shape=[jax.ShapeDtypeStruct((2, t), jnp.int32), jax.ShapeDtypeStruct((2, t), F32)],
        compiler_params=_cparams(("parallel",)),
        name="moe_router",
    )(x, w_hi, w_lo, bias_col)


def _gather_kernel(idx_ref, src_ref, dst_ref, sem, *, rows):
    base = pl.program_id(0) * rows

    def row_copy(r):
        return pltpu.make_async_copy(src_ref.at[pl.ds(idx_ref[0, r], 1), :],
                                     dst_ref.at[pl.ds(base + r, 1), :], sem)

    def issue(r, carry):
        row_copy(r).start()
        return carry

    def drain(r, carry):
        row_copy(r).wait()
        return carry

    lax.fori_loop(0, rows, issue, 0)
    lax.fori_loop(0, rows, drain, 0)


def _gather_rows(src, idx, rows):
    m = idx.shape[0]
    return pl.pallas_call(
        functools.partial(_gather_kernel, rows=rows),
        grid=(m // rows,),
        in_specs=[pl.BlockSpec((None, 1, rows), lambda i: (i, 0, 0), memory_space=pltpu.SMEM),
                  pl.BlockSpec(memory_space=pl.ANY)],
        out_specs=pl.BlockSpec(memory_space=pl.ANY),
        scratch_shapes=[pltpu.SemaphoreType.DMA(())],
        out_shape=jax.ShapeDtypeStruct((m, src.shape[1]), src.dtype),
        compiler_params=pltpu.CompilerParams(dimension_semantics=("arbitrary",)),
        name="row_gather",
    )(idx.reshape(m // rows, 1, rows), src)


def _expert_kernel(be_ref, nu_ref, *refs, mode):
    if mode == "plain":
        x_ref, w_ref, o_ref, wbf = refs
    elif mode == "swiglu":
        x_ref, w_ref, gate_ref, o_ref, wbf = refs
    else:
        x_ref, w_ref, rw_ref, o_ref, wbf = refs
    b = pl.program_id(0)
    prev = be_ref[jnp.maximum(b - 1, 0)]

    @pl.when((b == 0) | (be_ref[b] != prev))
    def _():
        wbf[...] = w_ref[0].astype(BF16)

    @pl.when(b < nu_ref[0])
    def _():
        y = jnp.dot(x_ref[...].astype(BF16), wbf[...], preferred_element_type=F32)
        if mode == "swiglu":
            gt = gate_ref[...].astype(F32)
            y = gt * jax.nn.sigmoid(gt) * y
        elif mode == "scale":
            y = y * rw_ref[...]
        o_ref[...] = y.astype(o_ref.dtype)

    @pl.when(b >= nu_ref[0])
    def _():
        o_ref[...] = jnp.zeros_like(o_ref)


def _expert_matmul(x, w, block_expert, n_used, extra, mode, out_dtype):
    n_rows, k = x.shape
    n = w.shape[2]
    nb = n_rows // MOE_BLOCK
    in_specs = [pl.BlockSpec((MOE_BLOCK, k), lambda b, be, nu: (b, 0)),
                pl.BlockSpec((1, k, n), lambda b, be, nu: (be[b], 0, 0))]
    args = [x, w]
    if mode == "swiglu":
        in_specs.append(pl.BlockSpec((MOE_BLOCK, n), lambda b, be, nu: (b, 0)))
        args.append(extra)
    elif mode == "scale":
        in_specs.append(pl.BlockSpec((MOE_BLOCK, 1), lambda b, be, nu: (b, 0)))
        args.append(extra)
    return pl.pallas_call(
        functools.partial(_expert_kernel, mode=mode),
        grid_spec=pltpu.PrefetchScalarGridSpec(
            num_scalar_prefetch=2, grid=(nb,), in_specs=in_specs,
            out_specs=pl.BlockSpec((MOE_BLOCK, n), lambda b, be, nu: (b, 0)),
            scratch_shapes=[pltpu.VMEM((k, n), BF16)]),
        out_shape=jax.ShapeDtypeStruct((n_rows, n), out_dtype),
        compiler_params=_cparams(("arbitrary",)),
        name="expert_matmul_" + mode,
    )(block_expert, n_used, *args)


def _combine_ln_kernel(x_ref, y0_ref, y1_ref, g_ref, b_ref, o_ref):
    o_ref[...] = _layernorm_rows(ALPHA * x_ref[...] + (y0_ref[...] + y1_ref[...]),
                                 g_ref[...], b_ref[...])


def _combine_ln(x, y2, ln_g, ln_b, tm):
    t = x.shape[0]
    nb = t // tm
    full = pl.BlockSpec((tm, D_MODEL), lambda i: (i, 0))
    vec = pl.BlockSpec((1, D_MODEL), lambda i: (0, 0))
    return pl.pallas_call(
        _combine_ln_kernel,
        grid=(nb,),
        in_specs=[full, full, pl.BlockSpec((tm, D_MODEL), lambda i: (i + nb, 0)), vec, vec],
        out_specs=full,
        out_shape=jax.ShapeDtypeStruct((t, D_MODEL), F32),
        compiler_params=_cparams(("parallel",)),
        name="moe_combine_ln",
    )(x, y2, y2, ln_g, ln_b)


def _moe(x, router, w_gate, w_up, w_down, ln_g, ln_b):
    t = x.shape[0]
    idx, gw = _route(x, *router, tm=512)
    n_assign = 2 * t
    flat_e = idx.reshape(-1)
    order = jnp.argsort(flat_e)
    se = flat_e[order]
    counts = jnp.bincount(flat_e, length=N_EXPERTS)
    padded = (counts + MOE_BLOCK - 1) // MOE_BLOCK * MOE_BLOCK
    pad_end = jnp.cumsum(padded)
    start = jnp.cumsum(counts) - counts
    dest = ((pad_end - padded)[se] + jnp.arange(n_assign) - start[se]).astype(jnp.int32)
    n_blocks = n_assign // MOE_BLOCK + N_EXPERTS
    n_rows = n_blocks * MOE_BLOCK
    row_tok = jnp.zeros((n_rows,), jnp.int32).at[dest].set((order % t).astype(jnp.int32))
    row_w = jnp.zeros((n_rows,), F32).at[dest].set(gw.reshape(-1)[order])
    pos = jnp.zeros((n_assign,), jnp.int32).at[order].set(dest)
    block_expert = jnp.minimum(
        jnp.searchsorted(pad_end, jnp.arange(n_blocks) * MOE_BLOCK, side='right'),
        N_EXPERTS - 1).astype(jnp.int32)
    n_used = (pad_end[-1:] // MOE_BLOCK).astype(jnp.int32)

    xg = _gather_rows(x, row_tok, rows=MOE_BLOCK)
    hg = _expert_matmul(xg, w_gate, block_expert, n_used, None, "plain", BF16)
    hh = _expert_matmul(xg, w_up, block_expert, n_used, hg, "swiglu", BF16)
    y_rows = _expert_matmul(hh, w_down, block_expert, n_used, row_w[:, None], "scale", F32)
    y2 = _gather_rows(y_rows, pos, rows=MOE_BLOCK)
    return _combine_ln(x, y2, ln_g, ln_b, tm=512)


def _trunk(x, mem, groups, w_in, w_out, lb_fwd, lb_bwd, hg_norm_w, rel_bias, wq_c, wk_c, wv_c, wo_c,
           router_w, router_bias, w_gate, w_up, w_down, ln1_g, ln1_b, ln2_g, ln2_b, ln3_g, ln3_b):
    lbf = _layer_lower_bounds(lb_fwd)
    lbb = _layer_lower_bounds(lb_bwd)
    bias_tab = _att_bias_table(rel_bias)
    rw_t = router_w.T.astype(F32)
    rw_hi = rw_t.astype(BF16)
    rw_lo = (rw_t - rw_hi.astype(F32)).astype(BF16)
    router = (rw_hi, rw_lo, router_bias.astype(F32)[:, None])
    row = lambda a: a.reshape(1, -1).astype(F32)
    for l in range(DEPTH):
        proj = _matmul(x, w_in[l].astype(BF16), 1024, 1024, F32)
        att = _dilated_attention(proj, bias_tab, groups)
        o_f, o_b = _hgrn_scan(proj, _gate_params(lbf[l], lbb[l]), groups)
        x = _mix_out(att, o_f, o_b, proj, x, w_out[l].astype(BF16), row(hg_norm_w[l]),
                     row(ln1_g[l]), row(ln1_b[l]), tm=256)
        w_kv = jnp.concatenate([wk_c[l], wv_c[l]], axis=1).astype(BF16)
        kv = _matmul(mem, w_kv, 512, 1024, BF16)
        x = _mem_attention(x, kv, wq_c[l].astype(BF16), wo_c[l].astype(BF16),
                           row(ln2_g[l]), row(ln2_b[l]), groups, tm=256)
        x = _moe(x, router, w_gate[l], w_up[l], w_down[l], row(ln3_g[l]), row(ln3_b[l]))
    return x


def kernel(x_prompt, x_sample, mem_prompt, mem_sample, w_in, w_out, lb_fwd, lb_bwd, hg_norm_w, rel_bias,
           wq_c, wk_c, wv_c, wo_c, router_w, router_bias, w_gate, w_up, w_down,
           ln1_g, ln1_b, ln2_g, ln2_b, ln3_g, ln3_b):
    n0, s0, d = x_prompt.shape
    n1, s1, _ = x_sample.shape
    groups = ((n0, s0), (n1, s1))
    x = jnp.concatenate([x_prompt.reshape(n0 * s0, d), x_sample.reshape(n1 * s1, d)], axis=0)
    mem = jnp.concatenate([mem_prompt.reshape(-1, d), mem_sample.reshape(-1, d)], axis=0)
    y = _trunk(x, mem, groups, w_in, w_out, lb_fwd, lb_bwd, hg_norm_w, rel_bias, wq_c, wk_c, wv_c, wo_c,
               router_w, router_bias, w_gate, w_up, w_down, ln1_g, ln1_b, ln2_g, ln2_b, ln3_g, ln3_b)
    return (y[:n0 * s0].reshape(n0, s0, d), y[n0 * s0:].reshape(n1, s1, d))
```

```python
import functools

import numpy as np
import jax
import jax.numpy as jnp
from jax import lax
from jax.experimental import pallas as pl
from jax.experimental.pallas import tpu as pltpu

F32 = jnp.float32
BF16 = jnp.bfloat16

D_MODEL = 2048
DEPTH = 2
ATT_WIDTH = 1024
ATT_HEAD_DIM = 128
ATT_HEADS = 8
DILATIONS = (1, 4, 16)
ATT_HALF = 64
N_REL_BUCKETS = 32
REL_MAX_DISTANCE = 1024
HG_WIDTH = 1024
HG_HEADS = 8
HG_DIM = 128
HG_CHUNK = 64
MEM_TOKENS = 256
MEM_HEADS = 4
MEM_HEAD_DIM = 512
N_EXPERTS = 64
N_GROUPS = 8
EXPERTS_PER_GROUP = 8
D_FF = 1408
ALPHA = (2 * DEPTH) ** 0.25
LN_EPS = 1e-5
RMS_EPS = 1e-6
NEG_INF = -1e30

LANES = 128
TOK_TILES = D_MODEL // LANES
ATT_BLOCK = 2048
ATT_PAD = 1024
ATT_QC = 128
ATT_KC = ATT_QC + 2 * ATT_HALF
HG_BLOCK = 512
MOE_BLOCK = 256
VMEM_LIMIT = 56 * 1024 * 1024

COL_AQ, COL_AK, COL_AV, COL_HQ, COL_HFF, COL_HFB, COL_HI, COL_HG = (0, 8, 16, 24, 32, 40, 48, 56)


def _cparams(sem):
    return pltpu.CompilerParams(dimension_semantics=sem, vmem_limit_bytes=VMEM_LIMIT)


def _seq_of_block(g, blk, groups):
    (n0, s0), (n1, s1) = groups
    nb0 = n0 * s0 // blk
    per0, per1 = s0 // blk, s1 // blk
    in0 = g < nb0
    first = jnp.where(in0, (g // per0) * per0, nb0 + ((g - nb0) // per1) * per1)
    last = first + jnp.where(in0, per0, per1) - 1
    slen = jnp.where(in0, s0, s1)
    return first, last, slen


def _mm_kernel(x_ref, w_ref, o_ref):
    o_ref[...] = jnp.dot(x_ref[...].astype(BF16), w_ref[...],
                         preferred_element_type=F32).astype(o_ref.dtype)


def _matmul(x, w, tm, tn, out_dtype):
    t, k = x.shape
    n = w.shape[1]
    return pl.pallas_call(
        _mm_kernel,
        grid=(t // tm, n // tn),
        in_specs=[pl.BlockSpec((tm, k), lambda i, j: (i, 0)),
                  pl.BlockSpec((k, tn), lambda i, j: (0, j))],
        out_specs=pl.BlockSpec((tm, tn), lambda i, j: (i, j)),
        out_shape=jax.ShapeDtypeStruct((t, n), out_dtype),
        compiler_params=_cparams(("parallel", "parallel")),
        name="dense_matmul",
    )(x, w)


def _rel_bucket(rel):
    nb = N_REL_BUCKETS // 2
    ret = (rel > 0).astype(np.int32) * nb
    n = np.abs(rel)
    max_exact = nb // 2
    large = max_exact + (np.log(np.maximum(n, 1) / max_exact) / np.log(REL_MAX_DISTANCE / max_exact)
                         * (nb - max_exact)).astype(np.int32)
    large = np.minimum(large, nb - 1)
    return ret + np.where(n < max_exact, n, large)


def _att_bias_table(rel_bias):
    off = np.arange(ATT_KC)[None, :] - np.arange(ATT_QC)[:, None] - ATT_HALF
    band = np.abs(off) <= ATT_HALF
    tabs = []
    for dil in DILATIONS:
        b = jnp.transpose(rel_bias[_rel_bucket(off * dil)], (2, 0, 1)).astype(F32)
        tabs.append(jnp.where(jnp.asarray(band)[None], b, NEG_INF))
    return jnp.stack(tabs)


def _att_kernel(q_ref, kp_ref, kc_ref, kn_ref, vp_ref, vc_ref, vn_ref, bias_ref, o_ref,
                kwin, vwin, acc_s, m_s, l_s, *, groups):
    g = pl.program_id(0)
    first, _, slen = _seq_of_block(g, ATT_BLOCK, groups)
    pos0 = (g - first) * ATT_BLOCK

    kwin[0:ATT_PAD, :] = kp_ref[ATT_BLOCK - ATT_PAD:, :]
    kwin[ATT_PAD:ATT_PAD + ATT_BLOCK, :] = kc_ref[...]
    kwin[ATT_PAD + ATT_BLOCK:, :] = kn_ref[0:ATT_PAD, :]
    vwin[0:ATT_PAD, :] = vp_ref[ATT_BLOCK - ATT_PAD:, :]
    vwin[ATT_PAD:ATT_PAD + ATT_BLOCK, :] = vc_ref[...]
    vwin[ATT_PAD + ATT_BLOCK:, :] = vn_ref[0:ATT_PAD, :]

    scale = ATT_HEAD_DIM ** -0.5
    col = lax.broadcasted_iota(jnp.int32, (1, ATT_KC), 1)
    n_chunks = ATT_BLOCK // ATT_QC
    for bi, dil in enumerate(DILATIONS):
        bias = bias_ref[bi, 0]
        for c in range(n_chunks):
            grp, ph = c // dil, c % dil
            qs = ATT_QC * dil * grp + ph
            ks = ATT_PAD + qs - ATT_HALF * dil
            rows_q = pl.ds(qs, ATT_QC, stride=dil) if dil > 1 else pl.ds(qs, ATT_QC)
            rows_k = pl.ds(ks, ATT_KC, stride=dil) if dil > 1 else pl.ds(ks, ATT_KC)
            qc = (q_ref[rows_q, :] * scale).astype(BF16)
            kc = kwin[rows_k, :].astype(BF16)
            vc = vwin[rows_k, :].astype(BF16)
            s = lax.dot_general(qc, kc, (((1,), (1,)), ((), ())), preferred_element_type=F32)
            s = s + bias
            kpos = pos0 + (ks - ATT_PAD) + dil * col
            s = jnp.where((kpos >= 0) & (kpos < slen), s, NEG_INF)
            m = jnp.max(s, axis=-1, keepdims=True)
            p = jnp.exp(s - m)
            l = jnp.sum(p, axis=-1, keepdims=True)
            o = jnp.dot(p.astype(BF16), vc, preferred_element_type=F32)
            acc_s[bi, rows_q, :] = o
            m_s[bi, rows_q, :] = jnp.broadcast_to(m, (ATT_QC, LANES))
            l_s[bi, rows_q, :] = jnp.broadcast_to(l, (ATT_QC, LANES))

    m = jnp.maximum(jnp.maximum(m_s[0], m_s[1]), m_s[2])
    num = jnp.zeros((ATT_BLOCK, LANES), F32)
    den = jnp.zeros((ATT_BLOCK, LANES), F32)
    for bi in range(len(DILATIONS)):
        w = jnp.exp(m_s[bi] - m)
        num = num + w * acc_s[bi]
        den = den + w * l_s[bi]
    o_ref[...] = num / den


def _dilated_attention(proj, bias_tab, groups):
    t = proj.shape[0]
    nblk = t // ATT_BLOCK
    blk = (ATT_BLOCK, LANES)

    def prev_map(col):
        def f(g, h):
            first, _, _ = _seq_of_block(g, ATT_BLOCK, groups)
            return (jnp.maximum(g - 1, first), col + h)
        return f

    def next_map(col):
        def f(g, h):
            _, last, _ = _seq_of_block(g, ATT_BLOCK, groups)
            return (jnp.minimum(g + 1, last), col + h)
        return f

    def cur_map(col):
        return lambda g, h: (g, col + h)

    n_br = len(DILATIONS)
    return pl.pallas_call(
        functools.partial(_att_kernel, groups=groups),
        grid=(nblk, ATT_HEADS),
        in_specs=[pl.BlockSpec(blk, cur_map(COL_AQ)),
                  pl.BlockSpec(blk, prev_map(COL_AK)), pl.BlockSpec(blk, cur_map(COL_AK)),
                  pl.BlockSpec(blk, next_map(COL_AK)),
                  pl.BlockSpec(blk, prev_map(COL_AV)), pl.BlockSpec(blk, cur_map(COL_AV)),
                  pl.BlockSpec(blk, next_map(COL_AV)),
                  pl.BlockSpec((n_br, 1, ATT_QC, ATT_KC), lambda g, h: (0, h, 0, 0))],
        out_specs=pl.BlockSpec(blk, lambda g, h: (g, h)),
        out_shape=jax.ShapeDtypeStruct((t, ATT_WIDTH), F32),
        scratch_shapes=[pltpu.VMEM((ATT_BLOCK + 2 * ATT_PAD, LANES), F32),
                        pltpu.VMEM((ATT_BLOCK + 2 * ATT_PAD, LANES), F32),
                        pltpu.VMEM((n_br, ATT_BLOCK, LANES), F32),
                        pltpu.VMEM((n_br, ATT_BLOCK, LANES), F32),
                        pltpu.VMEM((n_br, ATT_BLOCK, LANES), F32)],
        compiler_params=_cparams(("parallel", "parallel")),
        name="dilated_attention",
    )(proj, proj, proj, proj, proj, proj, proj, bias_tab)


HG_LEVELS = (32, 16, 8, 4, 2, 1)
HG_XROWS = HG_CHUNK * (2 + len(HG_LEVELS))


def _hg_constants(backward):
    c = HG_CHUNK
    t = np.arange(c)
    if backward:
        t = c - 1 - t
    tt, uu = t[:, None], t[None, :]
    mats = [(uu <= tt), (uu > tt)]
    masks, roles = [], []
    for h in HG_LEVELS:
        same = (tt // h) == (uu // h)
        upper = ((t // h) % 2 == 1)
        seg = np.where(upper[:, None], same & (uu <= tt), same & (uu > tt))
        mats.append(seg)
        pair = ((tt // (2 * h)) == (uu // (2 * h))) & upper[:, None] & (~upper)[None, :]
        masks.append(pair)
        roles.append(upper[:, None])
    masks.append(tt == uu)
    m = np.concatenate(mats, axis=0).astype(np.float32)
    m3 = np.concatenate([m, m, m], axis=1)
    return (jnp.asarray(m3, BF16), jnp.asarray(np.stack(masks), F32),
            jnp.asarray(np.stack(roles), F32))


def _hg_chunk(z, q, v, par, seg_ref, mask_ref, role_ref, st_ref, last_row):
    c = HG_CHUNK
    log_lb, log1m_lb, om_lb = par
    e = jnp.exp(-jnp.abs(z))
    log_sig = jnp.minimum(z, 0.0) - jnp.log1p(e)
    cc = log1m_lb + log_sig
    delta = jnp.minimum(jnp.abs(log_lb - cc), -NEG_INF)
    gl = jnp.maximum(log_lb, cc) + jnp.log1p(jnp.exp(-delta))
    k = om_lb * (jnp.where(z >= 0, e, 1.0) / (1.0 + e))

    g1 = gl.astype(BF16)
    r1 = gl - g1.astype(F32)
    g2 = r1.astype(BF16)
    g3 = (r1 - g2.astype(F32)).astype(BF16)
    sums = jnp.dot(seg_ref[...], jnp.concatenate([g1, g2, g3], axis=0),
                   preferred_element_type=F32)
    dec = jnp.exp(sums)

    st = st_ref[...]
    nt = (((1,), (1,)), ((), ()))
    o = lax.dot_general((q * dec[0:c]).astype(BF16), st.astype(BF16), nt,
                        preferred_element_type=F32)
    att = lax.dot_general(q.astype(BF16), k.astype(BF16), nt,
                          preferred_element_type=F32) * mask_ref[len(HG_LEVELS)]
    for li in range(len(HG_LEVELS)):
        y = (jnp.where(role_ref[li] > 0, q, k) * dec[(2 + li) * c:(3 + li) * c]).astype(BF16)
        att = att + lax.dot_general(y, y, nt, preferred_element_type=F32) * mask_ref[li]
    vb = v.astype(BF16)
    o = o + jnp.dot(att.astype(BF16), vb, preferred_element_type=F32)
    kd = (k * dec[c:2 * c]).astype(BF16)
    upd = lax.dot_general(vb, kd, (((0,), (0,)), ((), ())), preferred_element_type=F32)
    st_ref[...] = st * dec[last_row:last_row + 1] + upd
    return o


def _hgrn_kernel(qf_ref, zf_ref, vf_ref, qb_ref, zb_ref, vb_ref, par_ref,
                 segf_ref, maskf_ref, rolef_ref, segb_ref, maskb_ref, roleb_ref,
                 of_ref, ob_ref, stf_ref, stb_ref, *, groups, nblk):
    i = pl.program_id(1)
    j = nblk - 1 - i
    first_f, _, _ = _seq_of_block(i, HG_BLOCK, groups)
    _, last_b, _ = _seq_of_block(j, HG_BLOCK, groups)

    @pl.when(i == first_f)
    def _():
        stf_ref[...] = jnp.zeros_like(stf_ref)

    @pl.when(j == last_b)
    def _():
        stb_ref[...] = jnp.zeros_like(stb_ref)

    par = par_ref[0]
    par_f = (par[0:1], par[1:2], par[2:3])
    par_b = (par[3:4], par[4:5], par[5:6])
    n_chunks = HG_BLOCK // HG_CHUNK

    def body(ci, carry):
        rf = pl.ds(pl.multiple_of(ci * HG_CHUNK, HG_CHUNK), HG_CHUNK)
        rb = pl.ds(pl.multiple_of((n_chunks - 1 - ci) * HG_CHUNK, HG_CHUNK), HG_CHUNK)
        qf = qf_ref[rf, :]
        qb = qb_ref[rb, :]
        of_ref[rf, :] = _hg_chunk(zf_ref[rf, :], qf * jax.nn.sigmoid(qf), vf_ref[rf, :], par_f,
                                  segf_ref, maskf_ref, rolef_ref, stf_ref, HG_CHUNK - 1)
        ob_ref[rb, :] = _hg_chunk(zb_ref[rb, :], qb * jax.nn.sigmoid(qb), vb_ref[rb, :], par_b,
                                  segb_ref, maskb_ref, roleb_ref, stb_ref, 0)
        return carry

    lax.fori_loop(0, n_chunks, body, 0, unroll=2)


def _hgrn_scan(proj, gate_par, groups):
    t = proj.shape[0]
    nblk = t // HG_BLOCK
    blk = (HG_BLOCK, LANES)
    consts_f = _hg_constants(False)
    consts_b = _hg_constants(True)

    def fmap(col):
        return lambda h, i: (i, col + h)

    def bmap(col):
        return lambda h, i: (nblk - 1 - i, col + h)

    def const_spec(a):
        return pl.BlockSpec(a.shape, lambda h, i, nd=a.ndim: (0,) * nd)

    out = jax.ShapeDtypeStruct((t, HG_WIDTH), F32)
    return pl.pallas_call(
        functools.partial(_hgrn_kernel, groups=groups, nblk=nblk),
        grid=(HG_HEADS, nblk),
        in_specs=[pl.BlockSpec(blk, fmap(COL_HQ)), pl.BlockSpec(blk, fmap(COL_HFF)),
                  pl.BlockSpec(blk, fmap(COL_HI)),
                  pl.BlockSpec(blk, bmap(COL_HQ)), pl.BlockSpec(blk, bmap(COL_HFB)),
                  pl.BlockSpec(blk, bmap(COL_HI)),
                  pl.BlockSpec((1, 8, LANES), lambda h, i: (h, 0, 0))]
                 + [const_spec(a) for a in consts_f] + [const_spec(a) for a in consts_b],
        out_specs=[pl.BlockSpec(blk, lambda h, i: (i, h)),
                   pl.BlockSpec(blk, lambda h, i: (nblk - 1 - i, h))],
        out_shape=[out, out],
        scratch_shapes=[pltpu.VMEM((HG_DIM, HG_DIM), F32), pltpu.VMEM((HG_DIM, HG_DIM), F32)],
        compiler_params=_cparams(("parallel", "arbitrary")),
        name="hgrn2_scan",
    )(proj, proj, proj, proj, proj, proj, gate_par, *consts_f, *consts_b)


def _gate_params(lb_f, lb_b):
    rows = []
    for lb in (lb_f, lb_b):
        lb = lb.reshape(HG_HEADS, 1, HG_DIM)
        rows += [jnp.log(lb), jnp.log1p(-lb), 1.0 - lb]
    rows += [jnp.zeros_like(rows[0])] * 2
    return jnp.concatenate(rows, axis=1)


def _layer_lower_bounds(raw):
    p = jax.nn.softmax(raw.astype(F32), axis=0)
    c = jnp.cumsum(p, axis=0)
    return c - c[:1]


def _layernorm_rows(y, g, b):
    mu = jnp.mean(y, axis=-1, keepdims=True)
    d = y - mu
    var = jnp.mean(d * d, axis=-1, keepdims=True)
    return d * lax.rsqrt(var + LN_EPS) * g + b


def _mix_out_kernel(att_ref, of_ref, ob_ref, hg_ref, x_ref, w_ref, nw_ref, g_ref, b_ref, o_ref):
    o = of_ref[...] + ob_ref[...]
    nw = nw_ref[...]
    parts = []
    for h in range(HG_HEADS):
        oh = o[:, h * HG_DIM:(h + 1) * HG_DIM]
        parts.append(oh * lax.rsqrt(jnp.mean(oh * oh, axis=-1, keepdims=True) + RMS_EPS) * nw)
    hg = hg_ref[...]
    rec = jnp.concatenate(parts, axis=-1) * (hg * jax.nn.sigmoid(hg))
    h = jnp.dot(att_ref[...].astype(BF16), w_ref[0:ATT_WIDTH, :], preferred_element_type=F32)
    h = h + jnp.dot(rec.astype(BF16), w_ref[ATT_WIDTH:, :], preferred_element_type=F32)
    o_ref[...] = _layernorm_rows(ALPHA * x_ref[...] + h, g_ref[...], b_ref[...])


def _mix_out(att, o_f, o_b, proj, x, w_out, norm_w, ln_g, ln_b, tm):
    t = x.shape[0]
    half = pl.BlockSpec((tm, HG_WIDTH), lambda i: (i, 0))
    full = pl.BlockSpec((tm, D_MODEL), lambda i: (i, 0))
    vec = pl.BlockSpec((1, D_MODEL), lambda i: (0, 0))
    return pl.pallas_call(
        _mix_out_kernel,
        grid=(t // tm,),
        in_specs=[half, half, half,
                  pl.BlockSpec((tm, HG_WIDTH), lambda i: (i, COL_HG * LANES // HG_WIDTH)),
                  full,
                  pl.BlockSpec((D_MODEL, D_MODEL), lambda i: (0, 0), pipeline_mode=pl.Buffered(1)),
                  pl.BlockSpec((1, HG_DIM), lambda i: (0, 0)), vec, vec],
        out_specs=full,
        out_shape=jax.ShapeDtypeStruct((t, D_MODEL), F32),
        compiler_params=_cparams(("parallel",)),
        name="mixer_out_ln",
    )(att, o_f, o_b, proj, x, w_out, norm_w, ln_g, ln_b)


def _load_token_tiles(ref, rows):
    return jnp.concatenate([ref[pl.ds(j, rows, stride=TOK_TILES), :] for j in range(TOK_TILES)],
                           axis=1)


def _store_token_tiles(ref, val, rows):
    for j in range(TOK_TILES):
        ref[pl.ds(j, rows, stride=TOK_TILES), :] = val[:, j * LANES:(j + 1) * LANES]


def _mem_attn_kernel(x_ref, wq_ref, kv_ref, wo_ref, g_ref, b_ref, o_ref, ot_ref):
    x = x_ref[...]
    q = jnp.dot(x.astype(BF16), wq_ref[...], preferred_element_type=F32)
    scale = MEM_HEAD_DIM ** -0.5
    h_out = jnp.zeros(x.shape, F32)
    for h in range(MEM_HEADS):
        lo, hi = h * MEM_HEAD_DIM, (h + 1) * MEM_HEAD_DIM
        kh = kv_ref[:, lo:hi]
        vh = kv_ref[:, D_MODEL + lo:D_MODEL + hi]
        s = lax.dot_general(q[:, lo:hi].astype(BF16), kh, (((1,), (1,)), ((), ())),
                            preferred_element_type=F32) * scale
        e = jnp.exp(s - jnp.max(s, axis=-1, keepdims=True))
        p = e / jnp.sum(e, axis=-1, keepdims=True)
        oh = jnp.dot(p.astype(BF16), vh, preferred_element_type=F32)
        h_out = h_out + jnp.dot(oh.astype(BF16), wo_ref[lo:hi, :], preferred_element_type=F32)
    y = _layernorm_rows(ALPHA * x + h_out, g_ref[...], b_ref[...])
    o_ref[...] = y
    _store_token_tiles(ot_ref, y, x.shape[0])


def _mem_attention(x, kv, wq, wo, ln_g, ln_b, groups, tm):
    t = x.shape[0]
    (n0, s0), (n1, s1) = groups

    def kv_map(i):
        row = i * tm
        t0 = n0 * s0
        return (jnp.where(row < t0, row // s0, n0 + (row - t0) // s1), 0)

    full = pl.BlockSpec((tm, D_MODEL), lambda i: (i, 0))
    vec = pl.BlockSpec((1, D_MODEL), lambda i: (0, 0))
    wspec = pl.BlockSpec((D_MODEL, D_MODEL), lambda i: (0, 0), pipeline_mode=pl.Buffered(1))
    return pl.pallas_call(
        _mem_attn_kernel,
        grid=(t // tm,),
        in_specs=[full, wspec, pl.BlockSpec((MEM_TOKENS, 2 * D_MODEL), kv_map), wspec, vec, vec],
        out_specs=[full, pl.BlockSpec((tm * TOK_TILES, LANES), lambda i: (i, 0))],
        out_shape=[jax.ShapeDtypeStruct((t, D_MODEL), F32),
                   jax.ShapeDtypeStruct((t * TOK_TILES, LANES), F32)],
        compiler_params=_cparams(("parallel",)),
        name="memory_attention_ln",
    )(x, wq, kv, wo, ln_g, ln_b)


def _router_kernel(x_ref, whi_ref, wlo_ref, bias_ref, idx_ref, gw_ref):
    x = x_ref[...]
    x_hi = x.astype(BF16)
    x_lo = (x - x_hi.astype(F32)).astype(BF16)
    nt = (((1,), (1,)), ((), ()))
    whi = whi_ref[...]
    logits = (lax.dot_general(whi, x_hi, nt, preferred_element_type=F32)
              + lax.dot_general(whi, x_lo, nt, preferred_element_type=F32)
              + lax.dot_general(wlo_ref[...], x_hi, nt, preferred_element_type=F32))
    tm = logits.shape[1]
    scores = 1.0 / (1.0 + jnp.exp(-logits))
    biased = scores + bias_ref[...]
    g3 = biased.reshape(N_GROUPS, EXPERTS_PER_GROUP, tm)
    sub = lax.broadcasted_iota(jnp.int32, g3.shape, 1).astype(F32)
    m1 = jnp.max(g3, axis=1, keepdims=True)
    a1 = jnp.min(jnp.where(g3 == m1, sub, float(EXPERTS_PER_GROUP)), axis=1, keepdims=True)
    m2 = jnp.max(jnp.where(sub == a1, -jnp.inf, g3), axis=1, keepdims=True)
    gscore = (m1 + m2).reshape(N_GROUPS, tm)
    gi = lax.broadcasted_iota(jnp.int32, gscore.shape, 0).astype(F32)
    gmax = jnp.max(gscore, axis=0, keepdims=True)
    gsel = jnp.min(jnp.where(gscore == gmax, gi, float(N_GROUPS)), axis=0, keepdims=True)
    ei_int = lax.broadcasted_iota(jnp.int32, biased.shape, 0)
    ei = ei_int.astype(F32)
    egrp = (ei_int // EXPERTS_PER_GROUP).astype(F32)
    masked = jnp.where(egrp == gsel, biased, -jnp.inf)
    v1 = jnp.max(masked, axis=0, keepdims=True)
    i1 = jnp.min(jnp.where(masked == v1, ei, float(N_EXPERTS)), axis=0, keepdims=True)
    masked2 = jnp.where(ei == i1, -jnp.inf, masked)
    v2 = jnp.max(masked2, axis=0, keepdims=True)
    i2 = jnp.min(jnp.where(masked2 == v2, ei, float(N_EXPERTS)), axis=0, keepdims=True)
    w1 = jnp.sum(jnp.where(ei == i1, scores, 0.0), axis=0, keepdims=True)
    w2 = jnp.sum(jnp.where(ei == i2, scores, 0.0), axis=0, keepdims=True)
    tot = w1 + w2
    idx_ref[...] = jnp.concatenate([i1, i2], axis=0).astype(jnp.int32)
    gw_ref[...] = jnp.concatenate([w1 / tot, w2 / tot], axis=0)


def _route(x, w_hi, w_lo, bias_col, tm):
    t = x.shape[0]
    wspec = pl.BlockSpec((N_EXPERTS, D_MODEL), lambda i: (0, 0))
    return pl.pallas_call(
        _router_kernel,
        grid=(t // tm,),
        in_specs=[pl.BlockSpec((tm, D_MODEL), lambda i: (i, 0)), wspec, wspec,
                  pl.BlockSpec((N_EXPERTS, 1), lambda i: (0, 0))],
        out_specs=[pl.BlockSpec((2, tm), lambda i: (0, i)), pl.BlockSpec((2, tm), lambda i: (0, i))],
        out_shape=[jax.ShapeDtypeStruct((2, t), jnp.int32), jax.ShapeDtypeStruct((2, t), F32)],
        compiler_params=_cparams(("parallel",)),
        name="moe_router",
    )(x, w_hi, w_lo, bias_col)


def _gather_kernel(idx_ref, src_ref, dst_ref, sem, *, rows):
    base = pl.program_id(0) * rows

    def row_copy(r):
        return pltpu.make_async_copy(src_ref.at[idx_ref[0, r]], dst_ref.at[base + r], sem)

    def issue(r, carry):
        row_copy(r).start()
        return carry

    def drain(r, carry):
        row_copy(r).wait()
        return carry

    lax.fori_loop(0, rows, issue, 0)
    lax.fori_loop(0, rows, drain, 0)


def _gather_rows(src, idx, rows):
    m = idx.shape[0]
    src3 = src.reshape(src.shape[0] // TOK_TILES, TOK_TILES, LANES)
    out = pl.pallas_call(
        functools.partial(_gather_kernel, rows=rows),
        grid=(m // rows,),
        in_specs=[pl.BlockSpec((None, 1, rows), lambda i: (i, 0, 0), memory_space=pltpu.SMEM),
                  pl.BlockSpec(memory_space=pl.ANY)],
        out_specs=pl.BlockSpec(memory_space=pl.ANY),
        scratch_shapes=[pltpu.SemaphoreType.DMA(())],
        out_shape=jax.ShapeDtypeStruct((m, TOK_TILES, LANES), src.dtype),
        compiler_params=pltpu.CompilerParams(dimension_semantics=("arbitrary",)),
        name="row_gather",
    )(idx.reshape(m // rows, 1, rows), src3)
    return out.reshape(m * TOK_TILES, LANES)


def _expert_kernel(be_ref, nu_ref, *refs, mode):
    if mode == "up":
        x_ref, w_ref, gate_ref, o_ref, wbf = refs
    else:
        x_ref, w_ref, o_ref, wbf = refs
    b = pl.program_id(0)
    prev = be_ref[jnp.maximum(b - 1, 0)]

    @pl.when((b == 0) | (be_ref[b] != prev))
    def _():
        wbf[...] = w_ref[...].astype(BF16)

    @pl.when(b < nu_ref[0])
    def _():
        if mode == "down":
            x = x_ref[...]
        else:
            x = _load_token_tiles(x_ref, MOE_BLOCK).astype(BF16)
        y = jnp.dot(x, wbf[...], preferred_element_type=F32)
        if mode == "up":
            gt = gate_ref[...].astype(F32)
            y = gt * jax.nn.sigmoid(gt) * y
        if mode == "down":
            _store_token_tiles(o_ref, y, MOE_BLOCK)
        else:
            o_ref[...] = y.astype(o_ref.dtype)

    @pl.when(b >= nu_ref[0])
    def _():
        o_ref[...] = jnp.zeros_like(o_ref)


def _expert_matmul(x, w, layer, block_expert, n_used, gate, mode):
    _, _, k, n = w.shape
    if mode == "down":
        n_rows = x.shape[0]
        x_spec = pl.BlockSpec((MOE_BLOCK, k), lambda b, be, nu: (b, 0))
        out_spec = pl.BlockSpec((MOE_BLOCK * TOK_TILES, LANES), lambda b, be, nu: (b, 0))
        out_shape = jax.ShapeDtypeStruct((n_rows * TOK_TILES, LANES), F32)
    else:
        n_rows = x.shape[0] // TOK_TILES
        x_spec = pl.BlockSpec((MOE_BLOCK * TOK_TILES, LANES), lambda b, be, nu: (b, 0))
        out_spec = pl.BlockSpec((MOE_BLOCK, n), lambda b, be, nu: (b, 0))
        out_shape = jax.ShapeDtypeStruct((n_rows, n), BF16)
    in_specs = [x_spec, pl.BlockSpec((None, None, k, n), lambda b, be, nu: (layer, be[b], 0, 0))]
    args = [x, w]
    if mode == "up":
        in_specs.append(pl.BlockSpec((MOE_BLOCK, n), lambda b, be, nu: (b, 0)))
        args.append(gate)
    return pl.pallas_call(
        functools.partial(_expert_kernel, mode=mode),
        grid_spec=pltpu.PrefetchScalarGridSpec(
            num_scalar_prefetch=2, grid=(n_rows // MOE_BLOCK,), in_specs=in_specs,
            out_specs=out_spec, scratch_shapes=[pltpu.VMEM((k, n), BF16)]),
        out_shape=out_shape,
        compiler_params=_cparams(("arbitrary",)),
        name="expert_matmul_" + mode,
    )(block_expert, n_used, *args)


def _combine_ln_kernel(x_ref, y0_ref, y1_ref, gw_ref, g_ref, b_ref, o_ref):
    tm = x_ref.shape[0]
    gw = gw_ref[...]
    y = (_load_token_tiles(y0_ref, tm) * gw[:, 0:1]) + (_load_token_tiles(y1_ref, tm) * gw[:, 1:2])
    o_ref[...] = _layernorm_rows(ALPHA * x_ref[...] + y, g_ref[...], b_ref[...])


def _combine_ln(x, y2, gw_t, ln_g, ln_b, tm):
    t = x.shape[0]
    nb = t // tm
    full = pl.BlockSpec((tm, D_MODEL), lambda i: (i, 0))
    vec = pl.BlockSpec((1, D_MODEL), lambda i: (0, 0))
    return pl.pallas_call(
        _combine_ln_kernel,
        grid=(nb,),
        in_specs=[full,
                  pl.BlockSpec((tm * TOK_TILES, LANES), lambda i: (i, 0)),
                  pl.BlockSpec((tm * TOK_TILES, LANES), lambda i: (i + nb, 0)),
                  pl.BlockSpec((tm, 2), lambda i: (i, 0)), vec, vec],
        out_specs=full,
        out_shape=jax.ShapeDtypeStruct((t, D_MODEL), F32),
        compiler_params=_cparams(("parallel",)),
        name="moe_combine_ln",
    )(x, y2, y2, gw_t, ln_g, ln_b)


def _dispatch_plan(idx, t):
    n_assign = 2 * t
    n_blocks = n_assign // MOE_BLOCK + N_EXPERTS
    i32 = jnp.int32
    flat_e = idx.reshape(-1)
    ar = jnp.arange(n_assign, dtype=i32)
    _, order = lax.sort_key_val(flat_e, ar)
    _, inv = lax.sort_key_val(order, ar)
    onehot = flat_e[:, None] == jnp.arange(N_EXPERTS, dtype=i32)[None, :]
    counts = jnp.sum(onehot, axis=0, dtype=i32)
    padded = (counts + MOE_BLOCK - 1) // MOE_BLOCK * MOE_BLOCK
    pad_end = jnp.cumsum(padded)
    pad_start = pad_end - padded
    off = pad_start - (jnp.cumsum(counts) - counts)
    pos = inv + jnp.sum(jnp.where(onehot, off[None, :], 0), axis=1, dtype=i32)
    blk_start = jnp.arange(n_blocks, dtype=i32) * MOE_BLOCK
    block_expert = jnp.minimum(jnp.sum(pad_end[None, :] <= blk_start[:, None], axis=1, dtype=i32),
                               N_EXPERTS - 1)
    n_used = (pad_end[-1:] // MOE_BLOCK).astype(i32)
    rep = lambda a: jnp.repeat(a[block_expert], MOE_BLOCK)
    r = jnp.arange(n_blocks * MOE_BLOCK, dtype=i32)
    valid = (r - rep(pad_start)) < rep(counts)
    src = jnp.clip(r - rep(off), 0, n_assign - 1)
    row_tok = jnp.where(valid, order[src] % t, 0).astype(i32)
    return row_tok, pos.astype(i32), block_expert.astype(i32), n_used


def _moe(x, x_tok, layer, router, w_gate, w_up, w_down, ln_g, ln_b):
    t = x.shape[0]
    idx, gw = _route(x, *router, tm=512)
    row_tok, pos, block_expert, n_used = _dispatch_plan(idx, t)
    xg = _gather_rows(x_tok, row_tok, rows=MOE_BLOCK)
    hg = _expert_matmul(xg, w_gate, layer, block_expert, n_used, None, "gate")
    hh = _expert_matmul(xg, w_up, layer, block_expert, n_used, hg, "up")
    y_rows = _expert_matmul(hh, w_down, layer, block_expert, n_used, None, "down")
    y2 = _gather_rows(y_rows, pos, rows=MOE_BLOCK)
    return _combine_ln(x, y2, gw.T, ln_g, ln_b, tm=512)


def _trunk(x, mem, groups, w_in, w_out, lb_fwd, lb_bwd, hg_norm_w, rel_bias, wq_c, wk_c, wv_c, wo_c,
           router_w, router_bias, w_gate, w_up, w_down, ln1_g, ln1_b, ln2_g, ln2_b, ln3_g, ln3_b):
    lbf = _layer_lower_bounds(lb_fwd)
    lbb = _layer_lower_bounds(lb_bwd)
    bias_tab = _att_bias_table(rel_bias)
    rw_t = router_w.T.astype(F32)
    rw_hi = rw_t.astype(BF16)
    rw_lo = (rw_t - rw_hi.astype(F32)).astype(BF16)
    router = (rw_hi, rw_lo, router_bias.astype(F32)[:, None])
    row = lambda a: a.reshape(1, -1).astype(F32)
    for l in range(DEPTH):
        proj = _matmul(x, w_in[l].astype(BF16), 1024, 1024, F32)
        att = _dilated_attention(proj, bias_tab, groups)
        o_f, o_b = _hgrn_scan(proj, _gate_params(lbf[l], lbb[l]), groups)
        x = _mix_out(att, o_f, o_b, proj, x, w_out[l].astype(BF16), row(hg_norm_w[l]),
                     row(ln1_g[l]), row(ln1_b[l]), tm=256)
        w_kv = jnp.concatenate([wk_c[l], wv_c[l]], axis=1).astype(BF16)
        kv = _matmul(mem, w_kv, 512, 1024, BF16)
        x, x_tok = _mem_attention(x, kv, wq_c[l].astype(BF16), wo_c[l].astype(BF16),
                                  row(ln2_g[l]), row(ln2_b[l]), groups, tm=256)
        x = _moe(x, x_tok, l, router, w_gate, w_up, w_down, row(ln3_g[l]), row(ln3_b[l]))
    return x


def kernel(x_prompt, x_sample, mem_prompt, mem_sample, w_in, w_out, lb_fwd, lb_bwd, hg_norm_w, rel_bias,
           wq_c, wk_c, wv_c, wo_c, router_w, router_bias, w_gate, w_up, w_down,
           ln1_g, ln1_b, ln2_g, ln2_b, ln3_g, ln3_b):
    n0, s0, d = x_prompt.shape
    n1, s1, _ = x_sample.shape
    groups = ((n0, s0), (n1, s1))
    x = jnp.concatenate([x_prompt.reshape(n0 * s0, d), x_sample.reshape(n1 * s1, d)], axis=0)
    mem = jnp.concatenate([mem_prompt.reshape(-1, d), mem_sample.reshape(-1, d)], axis=0)
    y = _trunk(x, mem, groups, w_in, w_out, lb_fwd, lb_bwd, hg_norm_w, rel_bias, wq_c, wk_c, wv_c, wo_c,
               router_w, router_bias, w_gate, w_up, w_down, ln1_g, ln1_b, ln2_g, ln2_b, ln3_g, ln3_b)
    return (y[:n0 * s0].reshape(n0, s0, d), y[n0 * s0:].reshape(n1, s1, d))
```

```python
import functools

import numpy as np
import jax
import jax.numpy as jnp
from jax import lax
from jax.experimental import pallas as pl
from jax.experimental.pallas import tpu as pltpu

F32 = jnp.float32
BF16 = jnp.bfloat16

D_MODEL = 2048
DEPTH = 2
ATT_WIDTH = 1024
ATT_HEAD_DIM = 128
ATT_HEADS = 8
DILATIONS = (1, 4, 16)
ATT_HALF = 64
N_REL_BUCKETS = 32
REL_MAX_DISTANCE = 1024
HG_WIDTH = 1024
HG_HEADS = 8
HG_DIM = 128
HG_CHUNK = 64
MEM_TOKENS = 256
MEM_HEADS = 4
MEM_HEAD_DIM = 512
N_EXPERTS = 64
N_GROUPS = 8
EXPERTS_PER_GROUP = 8
D_FF = 1408
ALPHA = (2 * DEPTH) ** 0.25
LN_EPS = 1e-5
RMS_EPS = 1e-6
NEG_INF = -1e30

LANES = 128
TOK_TILES = D_MODEL // LANES
ATT_BLOCK = 2048
ATT_PAD = 1024
ATT_QC = 128
ATT_KC = ATT_QC + 2 * ATT_HALF
HG_BLOCK = 512
MOE_BLOCK = 256
VMEM_LIMIT = 56 * 1024 * 1024

COL_AQ, COL_AK, COL_AV, COL_HQ, COL_HFF, COL_HFB, COL_HI, COL_HG = (0, 8, 16, 24, 32, 40, 48, 56)


def _cparams(sem):
    return pltpu.CompilerParams(dimension_semantics=sem, vmem_limit_bytes=VMEM_LIMIT)


def _seq_of_block(g, blk, groups):
    (n0, s0), (n1, s1) = groups
    nb0 = n0 * s0 // blk
    per0, per1 = s0 // blk, s1 // blk
    in0 = g < nb0
    first = jnp.where(in0, (g // per0) * per0, nb0 + ((g - nb0) // per1) * per1)
    last = first + jnp.where(in0, per0, per1) - 1
    slen = jnp.where(in0, s0, s1)
    return first, last, slen


def _mm_kernel(x_ref, w_ref, o_ref):
    o_ref[...] = jnp.dot(x_ref[...].astype(BF16), w_ref[...],
                         preferred_element_type=F32).astype(o_ref.dtype)


def _matmul(x, w, tm, tn, out_dtype):
    t, k = x.shape
    n = w.shape[1]
    return pl.pallas_call(
        _mm_kernel,
        grid=(t // tm, n // tn),
        in_specs=[pl.BlockSpec((tm, k), lambda i, j: (i, 0)),
                  pl.BlockSpec((k, tn), lambda i, j: (0, j))],
        out_specs=pl.BlockSpec((tm, tn), lambda i, j: (i, j)),
        out_shape=jax.ShapeDtypeStruct((t, n), out_dtype),
        compiler_params=_cparams(("parallel", "parallel")),
        name="dense_matmul",
    )(x, w)


def _rel_bucket(rel):
    nb = N_REL_BUCKETS // 2
    ret = (rel > 0).astype(np.int32) * nb
    n = np.abs(rel)
    max_exact = nb // 2
    large = max_exact + (np.log(np.maximum(n, 1) / max_exact) / np.log(REL_MAX_DISTANCE / max_exact)
                         * (nb - max_exact)).astype(np.int32)
    large = np.minimum(large, nb - 1)
    return ret + np.where(n < max_exact, n, large)


def _att_bias_table(rel_bias):
    off = np.arange(ATT_KC)[None, :] - np.arange(ATT_QC)[:, None] - ATT_HALF
    band = np.abs(off) <= ATT_HALF
    tabs = []
    for dil in DILATIONS:
        b = jnp.transpose(rel_bias[_rel_bucket(off * dil)], (2, 0, 1)).astype(F32)
        tabs.append(jnp.where(jnp.asarray(band)[None], b, NEG_INF))
    return jnp.stack(tabs)


def _att_kernel(q_ref, kp_ref, kc_ref, kn_ref, vp_ref, vc_ref, vn_ref, bias_ref, o_ref,
                kwin, vwin, acc_s, m_s, l_s, *, groups):
    g = pl.program_id(0)
    first, _, slen = _seq_of_block(g, ATT_BLOCK, groups)
    pos0 = (g - first) * ATT_BLOCK

    kwin[0:ATT_PAD, :] = kp_ref[ATT_BLOCK - ATT_PAD:, :]
    kwin[ATT_PAD:ATT_PAD + ATT_BLOCK, :] = kc_ref[...]
    kwin[ATT_PAD + ATT_BLOCK:, :] = kn_ref[0:ATT_PAD, :]
    vwin[0:ATT_PAD, :] = vp_ref[ATT_BLOCK - ATT_PAD:, :]
    vwin[ATT_PAD:ATT_PAD + ATT_BLOCK, :] = vc_ref[...]
    vwin[ATT_PAD + ATT_BLOCK:, :] = vn_ref[0:ATT_PAD, :]

    scale = ATT_HEAD_DIM ** -0.5
    col = lax.broadcasted_iota(jnp.int32, (1, ATT_KC), 1)
    n_chunks = ATT_BLOCK // ATT_QC
    for bi, dil in enumerate(DILATIONS):
        bias = bias_ref[bi, 0]
        for c in range(n_chunks):
            grp, ph = c // dil, c % dil
            qs = ATT_QC * dil * grp + ph
            ks = ATT_PAD + qs - ATT_HALF * dil
            rows_q = pl.ds(qs, ATT_QC, stride=dil) if dil > 1 else pl.ds(qs, ATT_QC)
            rows_k = pl.ds(ks, ATT_KC, stride=dil) if dil > 1 else pl.ds(ks, ATT_KC)
            qc = (q_ref[rows_q, :] * scale).astype(BF16)
            kc = kwin[rows_k, :].astype(BF16)
            vc = vwin[rows_k, :].astype(BF16)
            s = lax.dot_general(qc, kc, (((1,), (1,)), ((), ())), preferred_element_type=F32)
            s = s + bias
            kpos = pos0 + (ks - ATT_PAD) + dil * col
            s = jnp.where((kpos >= 0) & (kpos < slen), s, NEG_INF)
            m = jnp.max(s, axis=-1, keepdims=True)
            p = jnp.exp(s - m)
            l = jnp.sum(p, axis=-1, keepdims=True)
            o = jnp.dot(p.astype(BF16), vc, preferred_element_type=F32)
            acc_s[bi, rows_q, :] = o
            m_s[bi, rows_q, :] = jnp.broadcast_to(m, (ATT_QC, LANES))
            l_s[bi, rows_q, :] = jnp.broadcast_to(l, (ATT_QC, LANES))

    m = jnp.maximum(jnp.maximum(m_s[0], m_s[1]), m_s[2])
    num = jnp.zeros((ATT_BLOCK, LANES), F32)
    den = jnp.zeros((ATT_BLOCK, LANES), F32)
    for bi in range(len(DILATIONS)):
        w = jnp.exp(m_s[bi] - m)
        num = num + w * acc_s[bi]
        den = den + w * l_s[bi]
    o_ref[...] = num / den


def _dilated_attention(proj, bias_tab, groups):
    t = proj.shape[0]
    nblk = t // ATT_BLOCK
    blk = (ATT_BLOCK, LANES)

    def prev_map(col):
        def f(g, h):
            first, _, _ = _seq_of_block(g, ATT_BLOCK, groups)
            return (jnp.maximum(g - 1, first), col + h)
        return f

    def next_map(col):
        def f(g, h):
            _, last, _ = _seq_of_block(g, ATT_BLOCK, groups)
            return (jnp.minimum(g + 1, last), col + h)
        return f

    def cur_map(col):
        return lambda g, h: (g, col + h)

    n_br = len(DILATIONS)
    return pl.pallas_call(
        functools.partial(_att_kernel, groups=groups),
        grid=(nblk, ATT_HEADS),
        in_specs=[pl.BlockSpec(blk, cur_map(COL_AQ)),
                  pl.BlockSpec(blk, prev_map(COL_AK)), pl.BlockSpec(blk, cur_map(COL_AK)),
                  pl.BlockSpec(blk, next_map(COL_AK)),
                  pl.BlockSpec(blk, prev_map(COL_AV)), pl.BlockSpec(blk, cur_map(COL_AV)),
                  pl.BlockSpec(blk, next_map(COL_AV)),
                  pl.BlockSpec((n_br, 1, ATT_QC, ATT_KC), lambda g, h: (0, h, 0, 0))],
        out_specs=pl.BlockSpec(blk, lambda g, h: (g, h)),
        out_shape=jax.ShapeDtypeStruct((t, ATT_WIDTH), F32),
        scratch_shapes=[pltpu.VMEM((ATT_BLOCK + 2 * ATT_PAD, LANES), F32),
                        pltpu.VMEM((ATT_BLOCK + 2 * ATT_PAD, LANES), F32),
                        pltpu.VMEM((n_br, ATT_BLOCK, LANES), F32),
                        pltpu.VMEM((n_br, ATT_BLOCK, LANES), F32),
                        pltpu.VMEM((n_br, ATT_BLOCK, LANES), F32)],
        compiler_params=_cparams(("parallel", "parallel")),
        name="dilated_attention",
    )(proj, proj, proj, proj, proj, proj, proj, bias_tab)


HG_LEVELS = (32, 16, 8, 4, 2, 1)
HG_XROWS = HG_CHUNK * (2 + len(HG_LEVELS))


def _hg_constants(backward):
    c = HG_CHUNK
    t = np.arange(c)
    if backward:
        t = c - 1 - t
    tt, uu = t[:, None], t[None, :]
    mats = [(uu <= tt), (uu > tt)]
    masks, roles = [], []
    for h in HG_LEVELS:
        same = (tt // h) == (uu // h)
        upper = ((t // h) % 2 == 1)
        seg = np.where(upper[:, None], same & (uu <= tt), same & (uu > tt))
        mats.append(seg)
        pair = ((tt // (2 * h)) == (uu // (2 * h))) & upper[:, None] & (~upper)[None, :]
        masks.append(pair)
        roles.append(upper[:, None])
    masks.append(tt == uu)
    m = np.concatenate(mats, axis=0).astype(np.float32)
    m3 = np.concatenate([m, m, m], axis=1)
    return (jnp.asarray(m3, BF16), jnp.asarray(np.stack(masks), F32),
            jnp.asarray(np.stack(roles), F32))


def _hg_chunk(z, q, v, par, seg_ref, mask_ref, role_ref, st_ref, last_row):
    c = HG_CHUNK
    log_lb, log1m_lb, om_lb = par
    e = jnp.exp(-jnp.abs(z))
    log_sig = jnp.minimum(z, 0.0) - jnp.log1p(e)
    cc = log1m_lb + log_sig
    delta = jnp.minimum(jnp.abs(log_lb - cc), -NEG_INF)
    gl = jnp.maximum(log_lb, cc) + jnp.log1p(jnp.exp(-delta))
    k = om_lb * (jnp.where(z >= 0, e, 1.0) / (1.0 + e))

    g1 = gl.astype(BF16)
    r1 = gl - g1.astype(F32)
    g2 = r1.astype(BF16)
    g3 = (r1 - g2.astype(F32)).astype(BF16)
    sums = jnp.dot(seg_ref[...], jnp.concatenate([g1, g2, g3], axis=0),
                   preferred_element_type=F32)
    dec = jnp.exp(sums)

    st = st_ref[...]
    nt = (((1,), (1,)), ((), ()))
    o = lax.dot_general((q * dec[0:c]).astype(BF16), st.astype(BF16), nt,
                        preferred_element_type=F32)
    att = lax.dot_general(q.astype(BF16), k.astype(BF16), nt,
                          preferred_element_type=F32) * mask_ref[len(HG_LEVELS)]
    for li in range(len(HG_LEVELS)):
        y = (jnp.where(role_ref[li] > 0, q, k) * dec[(2 + li) * c:(3 + li) * c]).astype(BF16)
        att = att + lax.dot_general(y, y, nt, preferred_element_type=F32) * mask_ref[li]
    vb = v.astype(BF16)
    o = o + jnp.dot(att.astype(BF16), vb, preferred_element_type=F32)
    kd = (k * dec[c:2 * c]).astype(BF16)
    upd = lax.dot_general(vb, kd, (((0,), (0,)), ((), ())), preferred_element_type=F32)
    st_ref[...] = st * dec[last_row:last_row + 1] + upd
    return o


def _hgrn_kernel(qf_ref, zf_ref, vf_ref, qb_ref, zb_ref, vb_ref, par_ref,
                 segf_ref, maskf_ref, rolef_ref, segb_ref, maskb_ref, roleb_ref,
                 of_ref, ob_ref, stf_ref, stb_ref, *, groups, nblk):
    i = pl.program_id(1)
    j = nblk - 1 - i
    first_f, _, _ = _seq_of_block(i, HG_BLOCK, groups)
    _, last_b, _ = _seq_of_block(j, HG_BLOCK, groups)

    @pl.when(i == first_f)
    def _():
        stf_ref[...] = jnp.zeros_like(stf_ref)

    @pl.when(j == last_b)
    def _():
        stb_ref[...] = jnp.zeros_like(stb_ref)

    par = par_ref[0]
    par_f = (par[0:1], par[1:2], par[2:3])
    par_b = (par[3:4], par[4:5], par[5:6])
    n_chunks = HG_BLOCK // HG_CHUNK

    def body(ci, carry):
        rf = pl.ds(pl.multiple_of(ci * HG_CHUNK, HG_CHUNK), HG_CHUNK)
        rb = pl.ds(pl.multiple_of((n_chunks - 1 - ci) * HG_CHUNK, HG_CHUNK), HG_CHUNK)
        qf = qf_ref[rf, :]
        qb = qb_ref[rb, :]
        of_ref[rf, :] = _hg_chunk(zf_ref[rf, :], qf * jax.nn.sigmoid(qf), vf_ref[rf, :], par_f,
                                  segf_ref, maskf_ref, rolef_ref, stf_ref, HG_CHUNK - 1)
        ob_ref[rb, :] = _hg_chunk(zb_ref[rb, :], qb * jax.nn.sigmoid(qb), vb_ref[rb, :], par_b,
                                  segb_ref, maskb_ref, roleb_ref, stb_ref, 0)
        return carry

    lax.fori_loop(0, n_chunks, body, 0, unroll=2)


def _hgrn_scan(proj, gate_par, groups):
    t = proj.shape[0]
    nblk = t // HG_BLOCK
    blk = (HG_BLOCK, LANES)
    consts_f = _hg_constants(False)
    consts_b = _hg_constants(True)

    def fmap(col):
        return lambda h, i: (i, col + h)

    def bmap(col):
        return lambda h, i: (nblk - 1 - i, col + h)

    def const_spec(a):
        return pl.BlockSpec(a.shape, lambda h, i, nd=a.ndim: (0,) * nd)

    out = jax.ShapeDtypeStruct((t, HG_WIDTH), F32)
    return pl.pallas_call(
        functools.partial(_hgrn_kernel, groups=groups, nblk=nblk),
        grid=(HG_HEADS, nblk),
        in_specs=[pl.BlockSpec(blk, fmap(COL_HQ)), pl.BlockSpec(blk, fmap(COL_HFF)),
                  pl.BlockSpec(blk, fmap(COL_HI)),
                  pl.BlockSpec(blk, bmap(COL_HQ)), pl.BlockSpec(blk, bmap(COL_HFB)),
                  pl.BlockSpec(blk, bmap(COL_HI)),
                  pl.BlockSpec((1, 8, LANES), lambda h, i: (h, 0, 0))]
                 + [const_spec(a) for a in consts_f] + [const_spec(a) for a in consts_b],
        out_specs=[pl.BlockSpec(blk, lambda h, i: (i, h)),
                   pl.BlockSpec(blk, lambda h, i: (nblk - 1 - i, h))],
        out_shape=[out, out],
        scratch_shapes=[pltpu.VMEM((HG_DIM, HG_DIM), F32), pltpu.VMEM((HG_DIM, HG_DIM), F32)],
        compiler_params=_cparams(("parallel", "arbitrary")),
        name="hgrn2_scan",
    )(proj, proj, proj, proj, proj, proj, gate_par, *consts_f, *consts_b)


def _gate_params(lb_f, lb_b):
    rows = []
    for lb in (lb_f, lb_b):
        lb = lb.reshape(HG_HEADS, 1, HG_DIM)
        rows += [jnp.log(lb), jnp.log1p(-lb), 1.0 - lb]
    rows += [jnp.zeros_like(rows[0])] * 2
    return jnp.concatenate(rows, axis=1)


def _layer_lower_bounds(raw):
    p = jax.nn.softmax(raw.astype(F32), axis=0)
    c = jnp.cumsum(p, axis=0)
    return c - c[:1]


def _layernorm_rows(y, g, b):
    mu = jnp.mean(y, axis=-1, keepdims=True)
    d = y - mu
    var = jnp.mean(d * d, axis=-1, keepdims=True)
    return d * lax.rsqrt(var + LN_EPS) * g + b


def _mix_out_kernel(att_ref, of_ref, ob_ref, hg_ref, x_ref, w_ref, nw_ref, g_ref, b_ref, o_ref):
    o = of_ref[...] + ob_ref[...]
    nw = nw_ref[...]
    parts = []
    for h in range(HG_HEADS):
        oh = o[:, h * HG_DIM:(h + 1) * HG_DIM]
        parts.append(oh * lax.rsqrt(jnp.mean(oh * oh, axis=-1, keepdims=True) + RMS_EPS) * nw)
    hg = hg_ref[...]
    rec = jnp.concatenate(parts, axis=-1) * (hg * jax.nn.sigmoid(hg))
    h = jnp.dot(att_ref[...].astype(BF16), w_ref[0:ATT_WIDTH, :], preferred_element_type=F32)
    h = h + jnp.dot(rec.astype(BF16), w_ref[ATT_WIDTH:, :], preferred_element_type=F32)
    o_ref[...] = _layernorm_rows(ALPHA * x_ref[...] + h, g_ref[...], b_ref[...])


def _mix_out(att, o_f, o_b, proj, x, w_out, norm_w, ln_g, ln_b, tm):
    t = x.shape[0]
    half = pl.BlockSpec((tm, HG_WIDTH), lambda i: (i, 0))
    full = pl.BlockSpec((tm, D_MODEL), lambda i: (i, 0))
    vec = pl.BlockSpec((1, D_MODEL), lambda i: (0, 0))
    return pl.pallas_call(
        _mix_out_kernel,
        grid=(t // tm,),
        in_specs=[half, half, half,
                  pl.BlockSpec((tm, HG_WIDTH), lambda i: (i, COL_HG * LANES // HG_WIDTH)),
                  full,
                  pl.BlockSpec((D_MODEL, D_MODEL), lambda i: (0, 0), pipeline_mode=pl.Buffered(1)),
                  pl.BlockSpec((1, HG_DIM), lambda i: (0, 0)), vec, vec],
        out_specs=full,
        out_shape=jax.ShapeDtypeStruct((t, D_MODEL), F32),
        compiler_params=_cparams(("parallel",)),
        name="mixer_out_ln",
    )(att, o_f, o_b, proj, x, w_out, norm_w, ln_g, ln_b)


def _load_token_tiles(ref, rows):
    return jnp.concatenate([ref[pl.ds(j, rows, stride=TOK_TILES), :] for j in range(TOK_TILES)],
                           axis=1)


def _store_token_tiles(ref, val, rows):
    for j in range(TOK_TILES):
        ref[pl.ds(j, rows, stride=TOK_TILES), :] = val[:, j * LANES:(j + 1) * LANES]


def _mem_attn_kernel(x_ref, wq_ref, kv_ref, wo_ref, g_ref, b_ref, o_ref, ot_ref):
    x = x_ref[...]
    q = jnp.dot(x.astype(BF16), wq_ref[...], preferred_element_type=F32)
    scale = MEM_HEAD_DIM ** -0.5
    h_out = jnp.zeros(x.shape, F32)
    for h in range(MEM_HEADS):
        lo, hi = h * MEM_HEAD_DIM, (h + 1) * MEM_HEAD_DIM
        kh = kv_ref[:, lo:hi]
        vh = kv_ref[:, D_MODEL + lo:D_MODEL + hi]
        s = lax.dot_general(q[:, lo:hi].astype(BF16), kh, (((1,), (1,)), ((), ())),
                            preferred_element_type=F32) * scale
        e = jnp.exp(s - jnp.max(s, axis=-1, keepdims=True))
        p = e / jnp.sum(e, axis=-1, keepdims=True)
        oh = jnp.dot(p.astype(BF16), vh, preferred_element_type=F32)
        h_out = h_out + jnp.dot(oh.astype(BF16), wo_ref[lo:hi, :], preferred_element_type=F32)
    y = _layernorm_rows(ALPHA * x + h_out, g_ref[...], b_ref[...])
    o_ref[...] = y
    _store_token_tiles(ot_ref, y, x.shape[0])


def _mem_attention(x, kv, wq, wo, ln_g, ln_b, groups, tm):
    t = x.shape[0]
    (n0, s0), (n1, s1) = groups

    def kv_map(i):
        row = i * tm
        t0 = n0 * s0
        return (jnp.where(row < t0, row // s0, n0 + (row - t0) // s1), 0)

    full = pl.BlockSpec((tm, D_MODEL), lambda i: (i, 0))
    vec = pl.BlockSpec((1, D_MODEL), lambda i: (0, 0))
    wspec = pl.BlockSpec((D_MODEL, D_MODEL), lambda i: (0, 0), pipeline_mode=pl.Buffered(1))
    return pl.pallas_call(
        _mem_attn_kernel,
        grid=(t // tm,),
        in_specs=[full, wspec, pl.BlockSpec((MEM_TOKENS, 2 * D_MODEL), kv_map), wspec, vec, vec],
        out_specs=[full, pl.BlockSpec((tm * TOK_TILES, LANES), lambda i: (i, 0))],
        out_shape=[jax.ShapeDtypeStruct((t, D_MODEL), F32),
                   jax.ShapeDtypeStruct((t * TOK_TILES, LANES), F32)],
        compiler_params=_cparams(("parallel",)),
        name="memory_attention_ln",
    )(x, wq, kv, wo, ln_g, ln_b)


def _router_kernel(x_ref, whi_ref, wlo_ref, bias_ref, idx_ref, gw_ref):
    x = x_ref[...]
    x_hi = x.astype(BF16)
    x_lo = (x - x_hi.astype(F32)).astype(BF16)
    nt = (((1,), (1,)), ((), ()))
    whi = whi_ref[...]
    logits = (lax.dot_general(whi, x_hi, nt, preferred_element_type=F32)
              + lax.dot_general(whi, x_lo, nt, preferred_element_type=F32)
              + lax.dot_general(wlo_ref[...], x_hi, nt, preferred_element_type=F32))
    tm = logits.shape[1]
    scores = 1.0 / (1.0 + jnp.exp(-logits))
    biased = scores + bias_ref[...]
    g3 = biased.reshape(N_GROUPS, EXPERTS_PER_GROUP, tm)
    sub = lax.broadcasted_iota(jnp.int32, g3.shape, 1).astype(F32)
    m1 = jnp.max(g3, axis=1, keepdims=True)
    a1 = jnp.min(jnp.where(g3 == m1, sub, float(EXPERTS_PER_GROUP)), axis=1, keepdims=True)
    m2 = jnp.max(jnp.where(sub == a1, -jnp.inf, g3), axis=1, keepdims=True)
    gscore = (m1 + m2).reshape(N_GROUPS, tm)
    gi = lax.broadcasted_iota(jnp.int32, gscore.shape, 0).astype(F32)
    gmax = jnp.max(gscore, axis=0, keepdims=True)
    gsel = jnp.min(jnp.where(gscore == gmax, gi, float(N_GROUPS)), axis=0, keepdims=True)
    ei_int = lax.broadcasted_iota(jnp.int32, biased.shape, 0)
    ei = ei_int.astype(F32)
    egrp = (ei_int // EXPERTS_PER_GROUP).astype(F32)
    masked = jnp.where(egrp == gsel, biased, -jnp.inf)
    v1 = jnp.max(masked, axis=0, keepdims=True)
    i1 = jnp.min(jnp.where(masked == v1, ei, float(N_EXPERTS)), axis=0, keepdims=True)
    masked2 = jnp.where(ei == i1, -jnp.inf, masked)
    v2 = jnp.max(masked2, axis=0, keepdims=True)
    i2 = jnp.min(jnp.where(masked2 == v2, ei, float(N_EXPERTS)), axis=0, keepdims=True)
    w1 = jnp.sum(jnp.where(ei == i1, scores, 0.0), axis=0, keepdims=True)
    w2 = jnp.sum(jnp.where(ei == i2, scores, 0.0), axis=0, keepdims=True)
    tot = w1 + w2
    idx_ref[...] = jnp.concatenate([i1, i2], axis=0).astype(jnp.int32)
    gw_ref[...] = jnp.concatenate([w1 / tot, w2 / tot], axis=0)


def _route(x, w_hi, w_lo, bias_col, tm):
    t = x.shape[0]
    wspec = pl.BlockSpec((N_EXPERTS, D_MODEL), lambda i: (0, 0))
    return pl.pallas_call(
        _router_kernel,
        grid=(t // tm,),
        in_specs=[pl.BlockSpec((tm, D_MODEL), lambda i: (i, 0)), wspec, wspec,
                  pl.BlockSpec((N_EXPERTS, 1), lambda i: (0, 0))],
        out_specs=[pl.BlockSpec((2, tm), lambda i: (0, i)), pl.BlockSpec((2, tm), lambda i: (0, i))],
        out_shape=[jax.ShapeDtypeStruct((2, t), jnp.int32), jax.ShapeDtypeStruct((2, t), F32)],
        compiler_params=_cparams(("parallel",)),
        name="moe_router",
    )(x, w_hi, w_lo, bias_col)


def _token_copy(src3_ref, idx_ref, dst_ref, sem, r):
    slot = pl.ds(pl.multiple_of(r * TOK_TILES, TOK_TILES), TOK_TILES)
    return pltpu.make_async_copy(src3_ref.at[idx_ref[0, r]], dst_ref.at[slot, :], sem)


def _gather_tokens(copies, rows):
    def issue(r, carry):
        for c in copies:
            c(r).start()
        return carry

    def drain(r, carry):
        for c in copies:
            c(r).wait()
        return carry

    lax.fori_loop(0, rows, issue, 0, unroll=8)
    lax.fori_loop(0, rows, drain, 0, unroll=8)


def _gather_kernel(idx_ref, src_ref, dst_ref, sem, *, rows):
    _gather_tokens([functools.partial(_token_copy, src_ref, idx_ref, dst_ref, sem)], rows)


def _gather_rows(src, idx, rows):
    m = idx.shape[0]
    src3 = src.reshape(src.shape[0] // TOK_TILES, TOK_TILES, LANES)
    return pl.pallas_call(
        functools.partial(_gather_kernel, rows=rows),
        grid=(m // rows,),
        in_specs=[pl.BlockSpec((None, 1, rows), lambda i: (i, 0, 0), memory_space=pltpu.SMEM),
                  pl.BlockSpec(memory_space=pl.ANY)],
        out_specs=pl.BlockSpec((rows * TOK_TILES, LANES), lambda i: (i, 0)),
        scratch_shapes=[pltpu.SemaphoreType.DMA(())],
        out_shape=jax.ShapeDtypeStruct((m * TOK_TILES, LANES), src.dtype),
        compiler_params=_cparams(("arbitrary",)),
        name="row_gather",
    )(idx.reshape(m // rows, 1, rows), src3)


def _expert_kernel(be_ref, nu_ref, *refs, mode):
    if mode == "up":
        x_ref, w_ref, gate_ref, o_ref, wbf = refs
    else:
        x_ref, w_ref, o_ref, wbf = refs
    b = pl.program_id(0)
    prev = be_ref[jnp.maximum(b - 1, 0)]

    @pl.when((b == 0) | (be_ref[b] != prev))
    def _():
        wbf[...] = w_ref[...].astype(BF16)

    @pl.when(b < nu_ref[0])
    def _():
        if mode == "down":
            x = x_ref[...]
        else:
            x = _load_token_tiles(x_ref, MOE_BLOCK).astype(BF16)
        y = jnp.dot(x, wbf[...], preferred_element_type=F32)
        if mode == "up":
            gt = gate_ref[...].astype(F32)
            y = gt * jax.nn.sigmoid(gt) * y
        if mode == "down":
            _store_token_tiles(o_ref, y, MOE_BLOCK)
        else:
            o_ref[...] = y.astype(o_ref.dtype)

    @pl.when(b >= nu_ref[0])
    def _():
        o_ref[...] = jnp.zeros_like(o_ref)


def _expert_matmul(x, w, layer, block_expert, n_used, gate, mode):
    _, _, k, n = w.shape
    if mode == "down":
        n_rows = x.shape[0]
        x_spec = pl.BlockSpec((MOE_BLOCK, k), lambda b, be, nu: (b, 0))
        out_spec = pl.BlockSpec((MOE_BLOCK * TOK_TILES, LANES), lambda b, be, nu: (b, 0))
        out_shape = jax.ShapeDtypeStruct((n_rows * TOK_TILES, LANES), F32)
    else:
        n_rows = x.shape[0] // TOK_TILES
        x_spec = pl.BlockSpec((MOE_BLOCK * TOK_TILES, LANES), lambda b, be, nu: (b, 0))
        out_spec = pl.BlockSpec((MOE_BLOCK, n), lambda b, be, nu: (b, 0))
        out_shape = jax.ShapeDtypeStruct((n_rows, n), BF16)
    in_specs = [x_spec, pl.BlockSpec((None, None, k, n), lambda b, be, nu: (layer, be[b], 0, 0))]
    args = [x, w]
    if mode == "up":
        in_specs.append(pl.BlockSpec((MOE_BLOCK, n), lambda b, be, nu: (b, 0)))
        args.append(gate)
    return pl.pallas_call(
        functools.partial(_expert_kernel, mode=mode),
        grid_spec=pltpu.PrefetchScalarGridSpec(
            num_scalar_prefetch=2, grid=(n_rows // MOE_BLOCK,), in_specs=in_specs,
            out_specs=out_spec, scratch_shapes=[pltpu.VMEM((k, n), BF16)]),
        out_shape=out_shape,
        compiler_params=_cparams(("arbitrary",)),
        name="expert_matmul_" + mode,
    )(block_expert, n_used, *args)


def _combine_ln_kernel(p0_ref, p1_ref, y_ref, x_ref, gw_ref, g_ref, b_ref, o_ref, buf0, buf1, sem):
    tm = x_ref.shape[0]
    _gather_tokens([functools.partial(_token_copy, y_ref, p0_ref, buf0, sem),
                    functools.partial(_token_copy, y_ref, p1_ref, buf1, sem)], tm)
    gw = gw_ref[...]
    y = (_load_token_tiles(buf0, tm) * gw[:, 0:1]) + (_load_token_tiles(buf1, tm) * gw[:, 1:2])
    o_ref[...] = _layernorm_rows(ALPHA * x_ref[...] + y, g_ref[...], b_ref[...])


def _combine_ln(x, y_rows, pos, gw_t, ln_g, ln_b, tm):
    t = x.shape[0]
    nb = t // tm
    full = pl.BlockSpec((tm, D_MODEL), lambda i: (i, 0))
    vec = pl.BlockSpec((1, D_MODEL), lambda i: (0, 0))
    pos4 = pos.reshape(2, nb, 1, tm)
    y3 = y_rows.reshape(y_rows.shape[0] // TOK_TILES, TOK_TILES, LANES)
    buf = pltpu.VMEM((tm * TOK_TILES, LANES), F32)
    return pl.pallas_call(
        _combine_ln_kernel,
        grid=(nb,),
        in_specs=[pl.BlockSpec((None, None, 1, tm), lambda i: (0, i, 0, 0), memory_space=pltpu.SMEM),
                  pl.BlockSpec((None, None, 1, tm), lambda i: (1, i, 0, 0), memory_space=pltpu.SMEM),
                  pl.BlockSpec(memory_space=pl.ANY),
                  full, pl.BlockSpec((tm, 2), lambda i: (i, 0)), vec, vec],
        out_specs=full,
        out_shape=jax.ShapeDtypeStruct((t, D_MODEL), F32),
        scratch_shapes=[buf, buf, pltpu.SemaphoreType.DMA(())],
        compiler_params=_cparams(("arbitrary",)),
        name="moe_combine_ln",
    )(pos4, pos4, y3, x, gw_t, ln_g, ln_b)


def _dispatch_plan(idx, t):
    n_assign = 2 * t
    n_blocks = n_assign // MOE_BLOCK + N_EXPERTS
    i32 = jnp.int32
    flat_e = idx.reshape(-1)
    ar = jnp.arange(n_assign, dtype=i32)
    _, order = lax.sort_key_val(flat_e, ar)
    _, inv = lax.sort_key_val(order, ar)
    onehot = flat_e[:, None] == jnp.arange(N_EXPERTS, dtype=i32)[None, :]
    counts = jnp.sum(onehot, axis=0, dtype=i32)
    padded = (counts + MOE_BLOCK - 1) // MOE_BLOCK * MOE_BLOCK
    pad_end = jnp.cumsum(padded)
    pad_start = pad_end - padded
    off = pad_start - (jnp.cumsum(counts) - counts)
    pos = inv + jnp.sum(jnp.where(onehot, off[None, :], 0), axis=1, dtype=i32)
    blk_start = jnp.arange(n_blocks, dtype=i32) * MOE_BLOCK
    block_expert = jnp.minimum(jnp.sum(pad_end[None, :] <= blk_start[:, None], axis=1, dtype=i32),
                               N_EXPERTS - 1)
    n_used = (pad_end[-1:] // MOE_BLOCK).astype(i32)
    rep = lambda a: jnp.repeat(a[block_expert], MOE_BLOCK)
    r = jnp.arange(n_blocks * MOE_BLOCK, dtype=i32)
    valid = (r - rep(pad_start)) < rep(counts)
    src = jnp.clip(r - rep(off), 0, n_assign - 1)
    row_tok = jnp.where(valid, order[src] % t, 0).astype(i32)
    return row_tok, pos.astype(i32), block_expert.astype(i32), n_used


def _moe(x, x_tok, layer, router, w_gate, w_up, w_down, ln_g, ln_b):
    t = x.shape[0]
    idx, gw = _route(x, *router, tm=512)
    row_tok, pos, block_expert, n_used = _dispatch_plan(idx, t)
    xg = _gather_rows(x_tok, row_tok, rows=MOE_BLOCK)
    hg = _expert_matmul(xg, w_gate, layer, block_expert, n_used, None, "gate")
    hh = _expert_matmul(xg, w_up, layer, block_expert, n_used, hg, "up")
    y_rows = _expert_matmul(hh, w_down, layer, block_expert, n_used, None, "down")
    return _combine_ln(x, y_rows, pos, gw.T, ln_g, ln_b, tm=512)


def _trunk(x, mem, groups, w_in, w_out, lb_fwd, lb_bwd, hg_norm_w, rel_bias, wq_c, wk_c, wv_c, wo_c,
           router_w, router_bias, w_gate, w_up, w_down, ln1_g, ln1_b, ln2_g, ln2_b, ln3_g, ln3_b):
    lbf = _layer_lower_bounds(lb_fwd)
    lbb = _layer_lower_bounds(lb_bwd)
    bias_tab = _att_bias_table(rel_bias)
    rw_t = router_w.T.astype(F32)
    rw_hi = rw_t.astype(BF16)
    rw_lo = (rw_t - rw_hi.astype(F32)).astype(BF16)
    router = (rw_hi, rw_lo, router_bias.astype(F32)[:, None])
    row = lambda a: a.reshape(1, -1).astype(F32)
    for l in range(DEPTH):
        proj = _matmul(x, w_in[l].astype(BF16), 1024, 1024, F32)
        att = _dilated_attention(proj, bias_tab, groups)
        o_f, o_b = _hgrn_scan(proj, _gate_params(lbf[l], lbb[l]), groups)
        x = _mix_out(att, o_f, o_b, proj, x, w_out[l].astype(BF16), row(hg_norm_w[l]),
                     row(ln1_g[l]), row(ln1_b[l]), tm=256)
        w_kv = jnp.concatenate([wk_c[l], wv_c[l]], axis=1).astype(BF16)
        kv = _matmul(mem, w_kv, 512, 1024, BF16)
        x, x_tok = _mem_attention(x, kv, wq_c[l].astype(BF16), wo_c[l].astype(BF16),
                                  row(ln2_g[l]), row(ln2_b[l]), groups, tm=256)
        x = _moe(x, x_tok, l, router, w_gate, w_up, w_down, row(ln3_g[l]), row(ln3_b[l]))
    return x


def kernel(x_prompt, x_sample, mem_prompt, mem_sample, w_in, w_out, lb_fwd, lb_bwd, hg_norm_w, rel_bias,
           wq_c, wk_c, wv_c, wo_c, router_w, router_bias, w_gate, w_up, w_down,
           ln1_g, ln1_b, ln2_g, ln2_b, ln3_g, ln3_b):
    n0, s0, d = x_prompt.shape
    n1, s1, _ = x_sample.shape
    groups = ((n0, s0), (n1, s1))
    x = jnp.concatenate([x_prompt.reshape(n0 * s0, d), x_sample.reshape(n1 * s1, d)], axis=0)
    mem = jnp.concatenate([mem_prompt.reshape(-1, d), mem_sample.reshape(-1, d)], axis=0)
    y = _trunk(x, mem, groups, w_in, w_out, lb_fwd, lb_bwd, hg_norm_w, rel_bias, wq_c, wk_c, wv_c, wo_c,
               router_w, router_bias, w_gate, w_up, w_down, ln1_g, ln1_b, ln2_g, ln2_b, ln3_g, ln3_b)
    return (y[:n0 * s0].reshape(n0, s0, d), y[n0 * s0:].reshape(n1, s1, d))
```

```python
import functools

import numpy as np
import jax
import jax.numpy as jnp
from jax import lax
from jax.experimental import pallas as pl
from jax.experimental.pallas import tpu as pltpu

F32 = jnp.float32
BF16 = jnp.bfloat16

D_MODEL = 2048
DEPTH = 2
ATT_WIDTH = 1024
ATT_HEAD_DIM = 128
ATT_HEADS = 8
DILATIONS = (1, 4, 16)
ATT_HALF = 64
N_REL_BUCKETS = 32
REL_MAX_DISTANCE = 1024
HG_WIDTH = 1024
HG_HEADS = 8
HG_DIM = 128
HG_CHUNK = 64
MEM_TOKENS = 256
MEM_HEADS = 4
MEM_HEAD_DIM = 512
N_EXPERTS = 64
N_GROUPS = 8
EXPERTS_PER_GROUP = 8
D_FF = 1408
ALPHA = (2 * DEPTH) ** 0.25
LN_EPS = 1e-5
RMS_EPS = 1e-6
NEG_INF = -1e30

LANES = 128
TOK_TILES = D_MODEL // LANES
ATT_BLOCK = 2048
ATT_PAD = 1024
ATT_QC = 128
ATT_KC = ATT_QC + 2 * ATT_HALF
HG_BLOCK = 512
MOE_BLOCK = 256
VMEM_LIMIT = 56 * 1024 * 1024

COL_AQ, COL_AK, COL_AV, COL_HQ, COL_HFF, COL_HFB, COL_HI, COL_HG = (0, 8, 16, 24, 32, 40, 48, 56)


def _cparams(sem):
    return pltpu.CompilerParams(dimension_semantics=sem, vmem_limit_bytes=VMEM_LIMIT)


def _seq_of_block(g, blk, groups):
    (n0, s0), (n1, s1) = groups
    nb0 = n0 * s0 // blk
    per0, per1 = s0 // blk, s1 // blk
    in0 = g < nb0
    first = jnp.where(in0, (g // per0) * per0, nb0 + ((g - nb0) // per1) * per1)
    last = first + jnp.where(in0, per0, per1) - 1
    slen = jnp.where(in0, s0, s1)
    return first, last, slen


def _mm_kernel(x_ref, w_ref, o_ref):
    o_ref[...] = jnp.dot(x_ref[...].astype(BF16), w_ref[...],
                         preferred_element_type=F32).astype(o_ref.dtype)


def _matmul(x, w, tm, tn, out_dtype):
    t, k = x.shape
    n = w.shape[1]
    return pl.pallas_call(
        _mm_kernel,
        grid=(t // tm, n // tn),
        in_specs=[pl.BlockSpec((tm, k), lambda i, j: (i, 0)),
                  pl.BlockSpec((k, tn), lambda i, j: (0, j))],
        out_specs=pl.BlockSpec((tm, tn), lambda i, j: (i, j)),
        out_shape=jax.ShapeDtypeStruct((t, n), out_dtype),
        compiler_params=_cparams(("parallel", "parallel")),
        name="dense_matmul",
    )(x, w)


def _rel_bucket(rel):
    nb = N_REL_BUCKETS // 2
    ret = (rel > 0).astype(np.int32) * nb
    n = np.abs(rel)
    max_exact = nb // 2
    large = max_exact + (np.log(np.maximum(n, 1) / max_exact) / np.log(REL_MAX_DISTANCE / max_exact)
                         * (nb - max_exact)).astype(np.int32)
    large = np.minimum(large, nb - 1)
    return ret + np.where(n < max_exact, n, large)


def _att_bias_table(rel_bias):
    off = np.arange(ATT_KC)[None, :] - np.arange(ATT_QC)[:, None] - ATT_HALF
    band = np.abs(off) <= ATT_HALF
    tabs = []
    for dil in DILATIONS:
        b = jnp.transpose(rel_bias[_rel_bucket(off * dil)], (2, 0, 1)).astype(F32)
        tabs.append(jnp.where(jnp.asarray(band)[None], b, NEG_INF))
    return jnp.stack(tabs)


def _att_kernel(q_ref, kp_ref, kc_ref, kn_ref, vp_ref, vc_ref, vn_ref, bias_ref, o_ref,
                kwin, vwin, acc_s, m_s, l_s, *, groups):
    g = pl.program_id(0)
    first, _, slen = _seq_of_block(g, ATT_BLOCK, groups)
    pos0 = (g - first) * ATT_BLOCK

    kwin[0:ATT_PAD, :] = kp_ref[ATT_BLOCK - ATT_PAD:, :]
    kwin[ATT_PAD:ATT_PAD + ATT_BLOCK, :] = kc_ref[...]
    kwin[ATT_PAD + ATT_BLOCK:, :] = kn_ref[0:ATT_PAD, :]
    vwin[0:ATT_PAD, :] = vp_ref[ATT_BLOCK - ATT_PAD:, :]
    vwin[ATT_PAD:ATT_PAD + ATT_BLOCK, :] = vc_ref[...]
    vwin[ATT_PAD + ATT_BLOCK:, :] = vn_ref[0:ATT_PAD, :]

    scale = ATT_HEAD_DIM ** -0.5
    col = lax.broadcasted_iota(jnp.int32, (1, ATT_KC), 1)
    n_chunks = ATT_BLOCK // ATT_QC
    for bi, dil in enumerate(DILATIONS):
        bias = bias_ref[bi, 0]
        for c in range(n_chunks):
            grp, ph = c // dil, c % dil
            qs = ATT_QC * dil * grp + ph
            ks = ATT_PAD + qs - ATT_HALF * dil
            rows_q = pl.ds(qs, ATT_QC, stride=dil) if dil > 1 else pl.ds(qs, ATT_QC)
            rows_k = pl.ds(ks, ATT_KC, stride=dil) if dil > 1 else pl.ds(ks, ATT_KC)
            qc = (q_ref[rows_q, :] * scale).astype(BF16)
            kc = kwin[rows_k, :].astype(BF16)
            vc = vwin[rows_k, :].astype(BF16)
            s = lax.dot_general(qc, kc, (((1,), (1,)), ((), ())), preferred_element_type=F32)
            s = s + bias
            kpos = pos0 + (ks - ATT_PAD) + dil * col
            s = jnp.where((kpos >= 0) & (kpos < slen), s, NEG_INF)
            m = jnp.max(s, axis=-1, keepdims=True)
            p = jnp.exp(s - m)
            l = jnp.sum(p, axis=-1, keepdims=True)
            o = jnp.dot(p.astype(BF16), vc, preferred_element_type=F32)
            acc_s[bi, rows_q, :] = o
            m_s[bi, rows_q, :] = jnp.broadcast_to(m, (ATT_QC, LANES))
            l_s[bi, rows_q, :] = jnp.broadcast_to(l, (ATT_QC, LANES))

    m = jnp.maximum(jnp.maximum(m_s[0], m_s[1]), m_s[2])
    num = jnp.zeros((ATT_BLOCK, LANES), F32)
    den = jnp.zeros((ATT_BLOCK, LANES), F32)
    for bi in range(len(DILATIONS)):
        w = jnp.exp(m_s[bi] - m)
        num = num + w * acc_s[bi]
        den = den + w * l_s[bi]
    o_ref[...] = num / den


def _dilated_attention(proj, bias_tab, groups):
    t = proj.shape[0]
    nblk = t // ATT_BLOCK
    blk = (ATT_BLOCK, LANES)

    def prev_map(col):
        def f(g, h):
            first, _, _ = _seq_of_block(g, ATT_BLOCK, groups)
            return (jnp.maximum(g - 1, first), col + h)
        return f

    def next_map(col):
        def f(g, h):
            _, last, _ = _seq_of_block(g, ATT_BLOCK, groups)
            return (jnp.minimum(g + 1, last), col + h)
        return f

    def cur_map(col):
        return lambda g, h: (g, col + h)

    n_br = len(DILATIONS)
    return pl.pallas_call(
        functools.partial(_att_kernel, groups=groups),
        grid=(nblk, ATT_HEADS),
        in_specs=[pl.BlockSpec(blk, cur_map(COL_AQ)),
                  pl.BlockSpec(blk, prev_map(COL_AK)), pl.BlockSpec(blk, cur_map(COL_AK)),
                  pl.BlockSpec(blk, next_map(COL_AK)),
                  pl.BlockSpec(blk, prev_map(COL_AV)), pl.BlockSpec(blk, cur_map(COL_AV)),
                  pl.BlockSpec(blk, next_map(COL_AV)),
                  pl.BlockSpec((n_br, 1, ATT_QC, ATT_KC), lambda g, h: (0, h, 0, 0))],
        out_specs=pl.BlockSpec(blk, lambda g, h: (g, h)),
        out_shape=jax.ShapeDtypeStruct((t, ATT_WIDTH), F32),
        scratch_shapes=[pltpu.VMEM((ATT_BLOCK + 2 * ATT_PAD, LANES), F32),
                        pltpu.VMEM((ATT_BLOCK + 2 * ATT_PAD, LANES), F32),
                        pltpu.VMEM((n_br, ATT_BLOCK, LANES), F32),
                        pltpu.VMEM((n_br, ATT_BLOCK, LANES), F32),
                        pltpu.VMEM((n_br, ATT_BLOCK, LANES), F32)],
        compiler_params=_cparams(("parallel", "parallel")),
        name="dilated_attention",
    )(proj, proj, proj, proj, proj, proj, proj, bias_tab)


HG_LEVELS = (32, 16, 8, 4, 2, 1)
HG_XROWS = HG_CHUNK * (2 + len(HG_LEVELS))


def _hg_constants(backward):
    c = HG_CHUNK
    t = np.arange(c)
    if backward:
        t = c - 1 - t
    tt, uu = t[:, None], t[None, :]
    mats = [(uu <= tt), (uu > tt)]
    masks, roles = [], []
    for h in HG_LEVELS:
        same = (tt // h) == (uu // h)
        upper = ((t // h) % 2 == 1)
        seg = np.where(upper[:, None], same & (uu <= tt), same & (uu > tt))
        mats.append(seg)
        pair = ((tt // (2 * h)) == (uu // (2 * h))) & upper[:, None] & (~upper)[None, :]
        masks.append(pair)
        roles.append(upper[:, None])
    masks.append(tt == uu)
    m = np.concatenate(mats, axis=0).astype(np.float32)
    m3 = np.concatenate([m, m, m], axis=1)
    return (jnp.asarray(m3, BF16), jnp.asarray(np.stack(masks), F32),
            jnp.asarray(np.stack(roles), F32))


def _hg_block(z_ref, q_ref, v_ref, par, seg_ref, mask_ref, role_ref, st_ref, o_ref, dec_s, q_s, k_s,
              backward):
    c = HG_CHUNK
    n = HG_BLOCK // c
    log_lb, log1m_lb, om_lb = par
    z = z_ref[...]
    e = jnp.exp(-jnp.abs(z))
    log_sig = jnp.minimum(z, 0.0) - jnp.log1p(e)
    cc = log1m_lb + log_sig
    delta = jnp.minimum(jnp.abs(log_lb - cc), -NEG_INF)
    gl = jnp.maximum(log_lb, cc) + jnp.log1p(jnp.exp(-delta))
    k = om_lb * (jnp.where(z >= 0, e, 1.0) / (1.0 + e))

    g1 = gl.astype(BF16)
    r1 = gl - g1.astype(F32)
    g2 = r1.astype(BF16)
    g3 = (r1 - g2.astype(F32)).astype(BF16)
    cols = [jnp.concatenate([g[i * c:(i + 1) * c] for g in (g1, g2, g3)], axis=0) for i in range(n)]
    sums = jnp.dot(seg_ref[...], jnp.concatenate(cols, axis=1),
                   preferred_element_type=F32)
    dec = jnp.exp(sums)
    for i in range(n):
        dec_s[i] = dec[:, i * LANES:(i + 1) * LANES]
    qr = q_ref[...]
    q_s[...] = qr * jax.nn.sigmoid(qr)
    k_s[...] = k

    nt = (((1,), (1,)), ((), ()))
    last_row = 0 if backward else c - 1
    st = st_ref[...]
    for i in (range(n - 1, -1, -1) if backward else range(n)):
        rows = pl.ds(i * c, c)
        q = q_s[rows, :]
        k = k_s[rows, :]
        att = lax.dot_general(q.astype(BF16), k.astype(BF16), nt,
                              preferred_element_type=F32) * mask_ref[len(HG_LEVELS)]
        for li in range(len(HG_LEVELS)):
            y = (jnp.where(role_ref[li] > 0, q, k) * dec_s[i, (2 + li) * c:(3 + li) * c, :]).astype(BF16)
            att = att + lax.dot_general(y, y, nt, preferred_element_type=F32) * mask_ref[li]
        vb = v_ref[rows, :].astype(BF16)
        o = lax.dot_general((q * dec_s[i, 0:c, :]).astype(BF16), st.astype(BF16), nt,
                            preferred_element_type=F32)
        o_ref[rows, :] = o + jnp.dot(att.astype(BF16), vb, preferred_element_type=F32)
        kd = (k * dec_s[i, c:2 * c, :]).astype(BF16)
        upd = lax.dot_general(vb, kd, (((0,), (0,)), ((), ())), preferred_element_type=F32)
        st = st * dec_s[i, last_row:last_row + 1, :] + upd
    st_ref[...] = st


def _hgrn_kernel(qf_ref, zf_ref, vf_ref, qb_ref, zb_ref, vb_ref, par_ref,
                 segf_ref, maskf_ref, rolef_ref, segb_ref, maskb_ref, roleb_ref,
                 of_ref, ob_ref, stf_ref, stb_ref, decf_s, decb_s, qf_s, kf_s, qb_s, kb_s,
                 *, groups, nblk):
    i = pl.program_id(1)
    j = nblk - 1 - i
    first_f, _, _ = _seq_of_block(i, HG_BLOCK, groups)
    _, last_b, _ = _seq_of_block(j, HG_BLOCK, groups)

    @pl.when(i == first_f)
    def _():
        stf_ref[...] = jnp.zeros_like(stf_ref)

    @pl.when(j == last_b)
    def _():
        stb_ref[...] = jnp.zeros_like(stb_ref)

    par = par_ref[0]
    par_f = (par[0:1], par[1:2], par[2:3])
    par_b = (par[3:4], par[4:5], par[5:6])
    _hg_block(zf_ref, qf_ref, vf_ref, par_f, segf_ref, maskf_ref, rolef_ref, stf_ref, of_ref,
              decf_s, qf_s, kf_s, backward=False)
    _hg_block(zb_ref, qb_ref, vb_ref, par_b, segb_ref, maskb_ref, roleb_ref, stb_ref, ob_ref,
              decb_s, qb_s, kb_s, backward=True)


def _hgrn_scan(proj, gate_par, groups):
    t = proj.shape[0]
    nblk = t // HG_BLOCK
    blk = (HG_BLOCK, LANES)
    consts_f = _hg_constants(False)
    consts_b = _hg_constants(True)

    def fmap(col):
        return lambda h, i: (i, col + h)

    def bmap(col):
        return lambda h, i: (nblk - 1 - i, col + h)

    def const_spec(a):
        return pl.BlockSpec(a.shape, lambda h, i, nd=a.ndim: (0,) * nd)

    out = jax.ShapeDtypeStruct((t, HG_WIDTH), F32)
    return pl.pallas_call(
        functools.partial(_hgrn_kernel, groups=groups, nblk=nblk),
        grid=(HG_HEADS, nblk),
        in_specs=[pl.BlockSpec(blk, fmap(COL_HQ)), pl.BlockSpec(blk, fmap(COL_HFF)),
                  pl.BlockSpec(blk, fmap(COL_HI)),
                  pl.BlockSpec(blk, bmap(COL_HQ)), pl.BlockSpec(blk, bmap(COL_HFB)),
                  pl.BlockSpec(blk, bmap(COL_HI)),
                  pl.BlockSpec((1, 8, LANES), lambda h, i: (h, 0, 0))]
                 + [const_spec(a) for a in consts_f] + [const_spec(a) for a in consts_b],
        out_specs=[pl.BlockSpec(blk, lambda h, i: (i, h)),
                   pl.BlockSpec(blk, lambda h, i: (nblk - 1 - i, h))],
        out_shape=[out, out],
        scratch_shapes=[pltpu.VMEM((HG_DIM, HG_DIM), F32), pltpu.VMEM((HG_DIM, HG_DIM), F32),
                        pltpu.VMEM((HG_BLOCK // HG_CHUNK, HG_XROWS, LANES), F32),
                        pltpu.VMEM((HG_BLOCK // HG_CHUNK, HG_XROWS, LANES), F32)]
                       + [pltpu.VMEM(blk, F32)] * 4,
        compiler_params=_cparams(("parallel", "arbitrary")),
        name="hgrn2_scan",
    )(proj, proj, proj, proj, proj, proj, gate_par, *consts_f, *consts_b)


def _gate_params(lb_f, lb_b):
    rows = []
    for lb in (lb_f, lb_b):
        lb = lb.reshape(HG_HEADS, 1, HG_DIM)
        rows += [jnp.log(lb), jnp.log1p(-lb), 1.0 - lb]
    rows += [jnp.zeros_like(rows[0])] * 2
    return jnp.concatenate(rows, axis=1)


def _layer_lower_bounds(raw):
    p = jax.nn.softmax(raw.astype(F32), axis=0)
    c = jnp.cumsum(p, axis=0)
    return c - c[:1]


def _layernorm_rows(y, g, b):
    mu = jnp.mean(y, axis=-1, keepdims=True)
    d = y - mu
    var = jnp.mean(d * d, axis=-1, keepdims=True)
    return d * lax.rsqrt(var + LN_EPS) * g + b


def _mix_out_kernel(att_ref, of_ref, ob_ref, hg_ref, x_ref, w_ref, nw_ref, g_ref, b_ref, o_ref):
    o = of_ref[...] + ob_ref[...]
    nw = nw_ref[...]
    parts = []
    for h in range(HG_HEADS):
        oh = o[:, h * HG_DIM:(h + 1) * HG_DIM]
        parts.append(oh * lax.rsqrt(jnp.mean(oh * oh, axis=-1, keepdims=True) + RMS_EPS) * nw)
    hg = hg_ref[...]
    rec = jnp.concatenate(parts, axis=-1) * (hg * jax.nn.sigmoid(hg))
    h = jnp.dot(att_ref[...].astype(BF16), w_ref[0:ATT_WIDTH, :], preferred_element_type=F32)
    h = h + jnp.dot(rec.astype(BF16), w_ref[ATT_WIDTH:, :], preferred_element_type=F32)
    o_ref[...] = _layernorm_rows(ALPHA * x_ref[...] + h, g_ref[...], b_ref[...])


def _mix_out(att, o_f, o_b, proj, x, w_out, norm_w, ln_g, ln_b, tm):
    t = x.shape[0]
    half = pl.BlockSpec((tm, HG_WIDTH), lambda i: (i, 0))
    full = pl.BlockSpec((tm, D_MODEL), lambda i: (i, 0))
    vec = pl.BlockSpec((1, D_MODEL), lambda i: (0, 0))
    return pl.pallas_call(
        _mix_out_kernel,
        grid=(t // tm,),
        in_specs=[half, half, half,
                  pl.BlockSpec((tm, HG_WIDTH), lambda i: (i, COL_HG * LANES // HG_WIDTH)),
                  full,
                  pl.BlockSpec((D_MODEL, D_MODEL), lambda i: (0, 0), pipeline_mode=pl.Buffered(1)),
                  pl.BlockSpec((1, HG_DIM), lambda i: (0, 0)), vec, vec],
        out_specs=full,
        out_shape=jax.ShapeDtypeStruct((t, D_MODEL), F32),
        compiler_params=_cparams(("parallel",)),
        name="mixer_out_ln",
    )(att, o_f, o_b, proj, x, w_out, norm_w, ln_g, ln_b)


def _load_token_tiles(ref, rows):
    return jnp.concatenate([ref[pl.ds(j, rows, stride=TOK_TILES), :] for j in range(TOK_TILES)],
                           axis=1)


def _store_token_tiles(ref, val, rows):
    for j in range(TOK_TILES):
        ref[pl.ds(j, rows, stride=TOK_TILES), :] = val[:, j * LANES:(j + 1) * LANES]


def _mem_attn_kernel(x_ref, wq_ref, kv_ref, wo_ref, g_ref, b_ref, o_ref, ot_ref):
    x = x_ref[...]
    q = jnp.dot(x.astype(BF16), wq_ref[...], preferred_element_type=F32)
    scale = MEM_HEAD_DIM ** -0.5
    h_out = jnp.zeros(x.shape, F32)
    for h in range(MEM_HEADS):
        lo, hi = h * MEM_HEAD_DIM, (h + 1) * MEM_HEAD_DIM
        kh = kv_ref[:, lo:hi]
        vh = kv_ref[:, D_MODEL + lo:D_MODEL + hi]
        s = lax.dot_general(q[:, lo:hi].astype(BF16), kh, (((1,), (1,)), ((), ())),
                            preferred_element_type=F32) * scale
        e = jnp.exp(s - jnp.max(s, axis=-1, keepdims=True))
        p = e / jnp.sum(e, axis=-1, keepdims=True)
        oh = jnp.dot(p.astype(BF16), vh, preferred_element_type=F32)
        h_out = h_out + jnp.dot(oh.astype(BF16), wo_ref[lo:hi, :], preferred_element_type=F32)
    y = _layernorm_rows(ALPHA * x + h_out, g_ref[...], b_ref[...])
    o_ref[...] = y
    _store_token_tiles(ot_ref, y, x.shape[0])


def _mem_attention(x, kv, wq, wo, ln_g, ln_b, groups, tm):
    t = x.shape[0]
    (n0, s0), (n1, s1) = groups

    def kv_map(i):
        row = i * tm
        t0 = n0 * s0
        return (jnp.where(row < t0, row // s0, n0 + (row - t0) // s1), 0)

    full = pl.BlockSpec((tm, D_MODEL), lambda i: (i, 0))
    vec = pl.BlockSpec((1, D_MODEL), lambda i: (0, 0))
    wspec = pl.BlockSpec((D_MODEL, D_MODEL), lambda i: (0, 0), pipeline_mode=pl.Buffered(1))
    return pl.pallas_call(
        _mem_attn_kernel,
        grid=(t // tm,),
        in_specs=[full, wspec, pl.BlockSpec((MEM_TOKENS, 2 * D_MODEL), kv_map), wspec, vec, vec],
        out_specs=[full, pl.BlockSpec((tm * TOK_TILES, LANES), lambda i: (i, 0))],
        out_shape=[jax.ShapeDtypeStruct((t, D_MODEL), F32),
                   jax.ShapeDtypeStruct((t * TOK_TILES, LANES), F32)],
        compiler_params=_cparams(("parallel",)),
        name="memory_attention_ln",
    )(x, wq, kv, wo, ln_g, ln_b)


def _router_kernel(x_ref, whi_ref, wlo_ref, bias_ref, idx_ref, gw_ref):
    x = x_ref[...]
    x_hi = x.astype(BF16)
    x_lo = (x - x_hi.astype(F32)).astype(BF16)
    nt = (((1,), (1,)), ((), ()))
    whi = whi_ref[...]
    logits = (lax.dot_general(whi, x_hi, nt, preferred_element_type=F32)
              + lax.dot_general(whi, x_lo, nt, preferred_element_type=F32)
              + lax.dot_general(wlo_ref[...], x_hi, nt, preferred_element_type=F32))
    tm = logits.shape[1]
    scores = 1.0 / (1.0 + jnp.exp(-logits))
    biased = scores + bias_ref[...]
    g3 = biased.reshape(N_GROUPS, EXPERTS_PER_GROUP, tm)
    sub = lax.broadcasted_iota(jnp.int32, g3.shape, 1).astype(F32)
    m1 = jnp.max(g3, axis=1, keepdims=True)
    a1 = jnp.min(jnp.where(g3 == m1, sub, float(EXPERTS_PER_GROUP)), axis=1, keepdims=True)
    m2 = jnp.max(jnp.where(sub == a1, -jnp.inf, g3), axis=1, keepdims=True)
    gscore = (m1 + m2).reshape(N_GROUPS, tm)
    gi = lax.broadcasted_iota(jnp.int32, gscore.shape, 0).astype(F32)
    gmax = jnp.max(gscore, axis=0, keepdims=True)
    gsel = jnp.min(jnp.where(gscore == gmax, gi, float(N_GROUPS)), axis=0, keepdims=True)
    ei_int = lax.broadcasted_iota(jnp.int32, biased.shape, 0)
    ei = ei_int.astype(F32)
    egrp = (ei_int // EXPERTS_PER_GROUP).astype(F32)
    masked = jnp.where(egrp == gsel, biased, -jnp.inf)
    v1 = jnp.max(masked, axis=0, keepdims=True)
    i1 = jnp.min(jnp.where(masked == v1, ei, float(N_EXPERTS)), axis=0, keepdims=True)
    masked2 = jnp.where(ei == i1, -jnp.inf, masked)
    v2 = jnp.max(masked2, axis=0, keepdims=True)
    i2 = jnp.min(jnp.where(masked2 == v2, ei, float(N_EXPERTS)), axis=0, keepdims=True)
    w1 = jnp.sum(jnp.where(ei == i1, scores, 0.0), axis=0, keepdims=True)
    w2 = jnp.sum(jnp.where(ei == i2, scores, 0.0), axis=0, keepdims=True)
    tot = w1 + w2
    idx_ref[...] = jnp.concatenate([i1, i2], axis=0).astype(jnp.int32)
    gw_ref[...] = jnp.concatenate([w1 / tot, w2 / tot], axis=0)


def _route(x, w_hi, w_lo, bias_col, tm):
    t = x.shape[0]
    wspec = pl.BlockSpec((N_EXPERTS, D_MODEL), lambda i: (0, 0))
    return pl.pallas_call(
        _router_kernel,
        grid=(t // tm,),
        in_specs=[pl.BlockSpec((tm, D_MODEL), lambda i: (i, 0)), wspec, wspec,
                  pl.BlockSpec((N_EXPERTS, 1), lambda i: (0, 0))],
        out_specs=[pl.BlockSpec((2, tm), lambda i: (0, i)), pl.BlockSpec((2, tm), lambda i: (0, i))],
        out_shape=[jax.ShapeDtypeStruct((2, t), jnp.int32), jax.ShapeDtypeStruct((2, t), F32)],
        compiler_params=_cparams(("parallel",)),
        name="moe_router",
    )(x, w_hi, w_lo, bias_col)


def _token_copy(src3_ref, idx_ref, dst_ref, sem, r):
    slot = pl.ds(pl.multiple_of(r * TOK_TILES, TOK_TILES), TOK_TILES)
    return pltpu.make_async_copy(src3_ref.at[idx_ref[0, r]], dst_ref.at[slot, :], sem)


def _start_tokens(copies, rows):
    def issue(r, carry):
        for c in copies:
            c(r).start()
        return carry

    lax.fori_loop(0, rows, issue, 0, unroll=8)


def _wait_tokens(copies, rows):
    def drain(r, carry):
        for c in copies:
            c(r).wait()
        return carry

    lax.fori_loop(0, rows, drain, 0, unroll=8)


def _expert_gate_kernel(be_ref, nu_ref, idx_ref, idx_next_ref, xtok_ref, w_ref, o_ref, xg_ref,
                        wbf, xbuf, sem):
    b = pl.program_id(0)
    cur = [functools.partial(_token_copy, xtok_ref, idx_ref, xbuf, sem)]
    nxt = [functools.partial(_token_copy, xtok_ref, idx_next_ref, xbuf, sem)]

    @pl.when(b == 0)
    def _():
        _start_tokens(cur, MOE_BLOCK)

    _wait_tokens(cur, MOE_BLOCK)
    xg_ref[...] = _load_token_tiles(xbuf, MOE_BLOCK).astype(BF16)

    @pl.when(b + 1 < pl.num_programs(0))
    def _():
        _start_tokens(nxt, MOE_BLOCK)

    prev = be_ref[jnp.maximum(b - 1, 0)]

    @pl.when((b == 0) | (be_ref[b] != prev))
    def _():
        wbf[...] = w_ref[...].astype(BF16)

    @pl.when(b < nu_ref[0])
    def _():
        o_ref[...] = jnp.dot(xg_ref[...], wbf[...], preferred_element_type=F32).astype(o_ref.dtype)

    @pl.when(b >= nu_ref[0])
    def _():
        o_ref[...] = jnp.zeros_like(o_ref)


def _expert_gate(x_tok, row_tok, w, layer, block_expert, n_used):
    _, _, k, n = w.shape
    n_rows = row_tok.shape[0]
    nb = n_rows // MOE_BLOCK
    idx3 = row_tok.reshape(nb, 1, MOE_BLOCK)
    x3 = x_tok.reshape(x_tok.shape[0] // TOK_TILES, TOK_TILES, LANES)
    idx_block = (None, 1, MOE_BLOCK)
    return pl.pallas_call(
        _expert_gate_kernel,
        grid_spec=pltpu.PrefetchScalarGridSpec(
            num_scalar_prefetch=2, grid=(nb,),
            in_specs=[pl.BlockSpec(idx_block, lambda b, be, nu: (b, 0, 0), memory_space=pltpu.SMEM),
                      pl.BlockSpec(idx_block, lambda b, be, nu: (jnp.minimum(b + 1, nb - 1), 0, 0),
                                   memory_space=pltpu.SMEM),
                      pl.BlockSpec(memory_space=pl.ANY),
                      pl.BlockSpec((None, None, k, n), lambda b, be, nu: (layer, be[b], 0, 0))],
            out_specs=[pl.BlockSpec((MOE_BLOCK, n), lambda b, be, nu: (b, 0)),
                       pl.BlockSpec((MOE_BLOCK, k), lambda b, be, nu: (b, 0))],
            scratch_shapes=[pltpu.VMEM((k, n), BF16),
                            pltpu.VMEM((MOE_BLOCK * TOK_TILES, LANES), F32),
                            pltpu.SemaphoreType.DMA(())]),
        out_shape=[jax.ShapeDtypeStruct((n_rows, n), BF16), jax.ShapeDtypeStruct((n_rows, k), BF16)],
        compiler_params=_cparams(("arbitrary",)),
        name="expert_gather_gate",
    )(block_expert, n_used, idx3, idx3, x3, w)


def _expert_kernel(be_ref, nu_ref, *refs, mode):
    if mode == "up":
        x_ref, w_ref, gate_ref, o_ref, wbf = refs
    else:
        x_ref, w_ref, o_ref, wbf = refs
    b = pl.program_id(0)
    prev = be_ref[jnp.maximum(b - 1, 0)]

    @pl.when((b == 0) | (be_ref[b] != prev))
    def _():
        wbf[...] = w_ref[...].astype(BF16)

    @pl.when(b < nu_ref[0])
    def _():
        y = jnp.dot(x_ref[...], wbf[...], preferred_element_type=F32)
        if mode == "up":
            gt = gate_ref[...].astype(F32)
            y = gt * jax.nn.sigmoid(gt) * y
        if mode == "down":
            _store_token_tiles(o_ref, y, MOE_BLOCK)
        else:
            o_ref[...] = y.astype(o_ref.dtype)

    @pl.when(b >= nu_ref[0])
    def _():
        o_ref[...] = jnp.zeros_like(o_ref)


def _expert_matmul(x, w, layer, block_expert, n_used, gate, mode):
    _, _, k, n = w.shape
    n_rows = x.shape[0]
    x_spec = pl.BlockSpec((MOE_BLOCK, k), lambda b, be, nu: (b, 0))
    if mode == "down":
        out_spec = pl.BlockSpec((MOE_BLOCK * TOK_TILES, LANES), lambda b, be, nu: (b, 0))
        out_shape = jax.ShapeDtypeStruct((n_rows * TOK_TILES, LANES), F32)
    else:
        out_spec = pl.BlockSpec((MOE_BLOCK, n), lambda b, be, nu: (b, 0))
        out_shape = jax.ShapeDtypeStruct((n_rows, n), BF16)
    in_specs = [x_spec, pl.BlockSpec((None, None, k, n), lambda b, be, nu: (layer, be[b], 0, 0))]
    args = [x, w]
    if mode == "up":
        in_specs.append(pl.BlockSpec((MOE_BLOCK, n), lambda b, be, nu: (b, 0)))
        args.append(gate)
    return pl.pallas_call(
        functools.partial(_expert_kernel, mode=mode),
        grid_spec=pltpu.PrefetchScalarGridSpec(
            num_scalar_prefetch=2, grid=(n_rows // MOE_BLOCK,), in_specs=in_specs,
            out_specs=out_spec, scratch_shapes=[pltpu.VMEM((k, n), BF16)]),
        out_shape=out_shape,
        compiler_params=_cparams(("arbitrary",)),
        name="expert_matmul_" + mode,
    )(block_expert, n_used, *args)


def _combine_ln_kernel(p0_ref, p1_ref, y_ref, x_ref, gw_ref, g_ref, b_ref, o_ref, buf0, buf1, sem):
    tm = x_ref.shape[0]
    copies = [functools.partial(_token_copy, y_ref, p0_ref, buf0, sem),
              functools.partial(_token_copy, y_ref, p1_ref, buf1, sem)]
    _start_tokens(copies, tm)
    _wait_tokens(copies, tm)
    gw = gw_ref[...]
    y = (_load_token_tiles(buf0, tm) * gw[:, 0:1]) + (_load_token_tiles(buf1, tm) * gw[:, 1:2])
    o_ref[...] = _layernorm_rows(ALPHA * x_ref[...] + y, g_ref[...], b_ref[...])


def _combine_ln(x, y_rows, pos, gw_t, ln_g, ln_b, tm):
    t = x.shape[0]
    nb = t // tm
    full = pl.BlockSpec((tm, D_MODEL), lambda i: (i, 0))
    vec = pl.BlockSpec((1, D_MODEL), lambda i: (0, 0))
    pos4 = pos.reshape(2, nb, 1, tm)
    y3 = y_rows.reshape(y_rows.shape[0] // TOK_TILES, TOK_TILES, LANES)
    buf = pltpu.VMEM((tm * TOK_TILES, LANES), F32)
    return pl.pallas_call(
        _combine_ln_kernel,
        grid=(nb,),
        in_specs=[pl.BlockSpec((None, None, 1, tm), lambda i: (0, i, 0, 0), memory_space=pltpu.SMEM),
                  pl.BlockSpec((None, None, 1, tm), lambda i: (1, i, 0, 0), memory_space=pltpu.SMEM),
                  pl.BlockSpec(memory_space=pl.ANY),
                  full, pl.BlockSpec((tm, 2), lambda i: (i, 0)), vec, vec],
        out_specs=full,
        out_shape=jax.ShapeDtypeStruct((t, D_MODEL), F32),
        scratch_shapes=[buf, buf, pltpu.SemaphoreType.DMA(())],
        compiler_params=_cparams(("arbitrary",)),
        name="moe_combine_ln",
    )(pos4, pos4, y3, x, gw_t, ln_g, ln_b)


def _dispatch_plan(idx, t):
    n_assign = 2 * t
    n_blocks = n_assign // MOE_BLOCK + N_EXPERTS
    i32 = jnp.int32
    flat_e = idx.reshape(-1)
    ar = jnp.arange(n_assign, dtype=i32)
    _, order = lax.sort_key_val(flat_e, ar)
    _, inv = lax.sort_key_val(order, ar)
    onehot = flat_e[:, None] == jnp.arange(N_EXPERTS, dtype=i32)[None, :]
    counts = jnp.sum(onehot, axis=0, dtype=i32)
    padded = (counts + MOE_BLOCK - 1) // MOE_BLOCK * MOE_BLOCK
    pad_end = jnp.cumsum(padded)
    pad_start = pad_end - padded
    off = pad_start - (jnp.cumsum(counts) - counts)
    pos = inv + jnp.sum(jnp.where(onehot, off[None, :], 0), axis=1, dtype=i32)
    blk_start = jnp.arange(n_blocks, dtype=i32) * MOE_BLOCK
    block_expert = jnp.minimum(jnp.sum(pad_end[None, :] <= blk_start[:, None], axis=1, dtype=i32),
                               N_EXPERTS - 1)
    n_used = (pad_end[-1:] // MOE_BLOCK).astype(i32)
    rep = lambda a: jnp.repeat(a[block_expert], MOE_BLOCK)
    r = jnp.arange(n_blocks * MOE_BLOCK, dtype=i32)
    valid = (r - rep(pad_start)) < rep(counts)
    src = jnp.clip(r - rep(off), 0, n_assign - 1)
    row_tok = jnp.where(valid, order[src] % t, 0).astype(i32)
    return row_tok, pos.astype(i32), block_expert.astype(i32), n_used


def _moe(x, x_tok, layer, router, w_gate, w_up, w_down, ln_g, ln_b):
    t = x.shape[0]
    idx, gw = _route(x, *router, tm=512)
    row_tok, pos, block_expert, n_used = _dispatch_plan(idx, t)
    hg, xg = _expert_gate(x_tok, row_tok, w_gate, layer, block_expert, n_used)
    hh = _expert_matmul(xg, w_up, layer, block_expert, n_used, hg, "up")
    y_rows = _expert_matmul(hh, w_down, layer, block_expert, n_used, None, "down")
    return _combine_ln(x, y_rows, pos, gw.T, ln_g, ln_b, tm=512)


def _trunk(x, mem, groups, w_in, w_out, lb_fwd, lb_bwd, hg_norm_w, rel_bias, wq_c, wk_c, wv_c, wo_c,
           router_w, router_bias, w_gate, w_up, w_down, ln1_g, ln1_b, ln2_g, ln2_b, ln3_g, ln3_b):
    lbf = _layer_lower_bounds(lb_fwd)
    lbb = _layer_lower_bounds(lb_bwd)
    bias_tab = _att_bias_table(rel_bias)
    rw_t = router_w.T.astype(F32)
    rw_hi = rw_t.astype(BF16)
    rw_lo = (rw_t - rw_hi.astype(F32)).astype(BF16)
    router = (rw_hi, rw_lo, router_bias.astype(F32)[:, None])
    row = lambda a: a.reshape(1, -1).astype(F32)
    for l in range(DEPTH):
        proj = _matmul(x, w_in[l].astype(BF16), 1024, 1024, F32)
        att = _dilated_attention(proj, bias_tab, groups)
        o_f, o_b = _hgrn_scan(proj, _gate_params(lbf[l], lbb[l]), groups)
        x = _mix_out(att, o_f, o_b, proj, x, w_out[l].astype(BF16), row(hg_norm_w[l]),
                     row(ln1_g[l]), row(ln1_b[l]), tm=256)
        w_kv = jnp.concatenate([wk_c[l], wv_c[l]], axis=1).astype(BF16)
        kv = _matmul(mem, w_kv, 512, 1024, BF16)
        x, x_tok = _mem_attention(x, kv, wq_c[l].astype(BF16), wo_c[l].astype(BF16),
                                  row(ln2_g[l]), row(ln2_b[l]), groups, tm=256)
        x = _moe(x, x_tok, l, router, w_gate, w_up, w_down, row(ln3_g[l]), row(ln3_b[l]))
    return x


def kernel(x_prompt, x_sample, mem_prompt, mem_sample, w_in, w_out, lb_fwd, lb_bwd, hg_norm_w, rel_bias,
           wq_c, wk_c, wv_c, wo_c, router_w, router_bias, w_gate, w_up, w_down,
           ln1_g, ln1_b, ln2_g, ln2_b, ln3_g, ln3_b):
    n0, s0, d = x_prompt.shape
    n1, s1, _ = x_sample.shape
    groups = ((n0, s0), (n1, s1))
    x = jnp.concatenate([x_prompt.reshape(n0 * s0, d), x_sample.reshape(n1 * s1, d)], axis=0)
    mem = jnp.concatenate([mem_prompt.reshape(-1, d), mem_sample.reshape(-1, d)], axis=0)
    y = _trunk(x, mem, groups, w_in, w_out, lb_fwd, lb_bwd, hg_norm_w, rel_bias, wq_c, wk_c, wv_c, wo_c,
               router_w, router_bias, w_gate, w_up, w_down, ln1_g, ln1_b, ln2_g, ln2_b, ln3_g, ln3_b)
    return (y[:n0 * s0].reshape(n0, s0, d), y[n0 * s0:].reshape(n1, s1, d))
```

```python
import functools

import numpy as np
import jax
import jax.numpy as jnp
from jax import lax
from jax.experimental import pallas as pl
from jax.experimental.pallas import tpu as pltpu

F32 = jnp.float32
BF16 = jnp.bfloat16

D_MODEL = 2048
DEPTH = 2
ATT_WIDTH = 1024
ATT_HEAD_DIM = 128
ATT_HEADS = 8
DILATIONS = (1, 4, 16)
ATT_HALF = 64
N_REL_BUCKETS = 32
REL_MAX_DISTANCE = 1024
HG_WIDTH = 1024
HG_HEADS = 8
HG_DIM = 128
HG_CHUNK = 64
MEM_TOKENS = 256
MEM_HEADS = 4
MEM_HEAD_DIM = 512
N_EXPERTS = 64
N_GROUPS = 8
EXPERTS_PER_GROUP = 8
D_FF = 1408
ALPHA = (2 * DEPTH) ** 0.25
LN_EPS = 1e-5
RMS_EPS = 1e-6
NEG_INF = -1e30

LANES = 128
TOK_TILES = D_MODEL // LANES
ATT_BLOCK = 2048
ATT_PAD = 1024
ATT_QC = 128
ATT_KC = ATT_QC + 2 * ATT_HALF
HG_BLOCK = 512
MOE_BLOCK = 256
VMEM_LIMIT = 56 * 1024 * 1024

COL_AQ, COL_AK, COL_AV, COL_HQ, COL_HFF, COL_HFB, COL_HI, COL_HG = (0, 8, 16, 24, 32, 40, 48, 56)


def _cparams(sem):
    return pltpu.CompilerParams(dimension_semantics=sem, vmem_limit_bytes=VMEM_LIMIT)


def _seq_of_block(g, blk, groups):
    (n0, s0), (n1, s1) = groups
    nb0 = n0 * s0 // blk
    per0, per1 = s0 // blk, s1 // blk
    in0 = g < nb0
    first = jnp.where(in0, (g // per0) * per0, nb0 + ((g - nb0) // per1) * per1)
    last = first + jnp.where(in0, per0, per1) - 1
    slen = jnp.where(in0, s0, s1)
    return first, last, slen


def _mm_kernel(x_ref, w_ref, o_ref):
    o_ref[...] = jnp.dot(x_ref[...].astype(BF16), w_ref[...],
                         preferred_element_type=F32).astype(o_ref.dtype)


def _matmul(x, w, tm, tn, out_dtype):
    t, k = x.shape
    n = w.shape[1]
    return pl.pallas_call(
        _mm_kernel,
        grid=(t // tm, n // tn),
        in_specs=[pl.BlockSpec((tm, k), lambda i, j: (i, 0)),
                  pl.BlockSpec((k, tn), lambda i, j: (0, j))],
        out_specs=pl.BlockSpec((tm, tn), lambda i, j: (i, j)),
        out_shape=jax.ShapeDtypeStruct((t, n), out_dtype),
        compiler_params=_cparams(("parallel", "parallel")),
        name="dense_matmul",
    )(x, w)


def _rel_bucket(rel):
    nb = N_REL_BUCKETS // 2
    ret = (rel > 0).astype(np.int32) * nb
    n = np.abs(rel)
    max_exact = nb // 2
    large = max_exact + (np.log(np.maximum(n, 1) / max_exact) / np.log(REL_MAX_DISTANCE / max_exact)
                         * (nb - max_exact)).astype(np.int32)
    large = np.minimum(large, nb - 1)
    return ret + np.where(n < max_exact, n, large)


def _att_bias_table(rel_bias):
    off = np.arange(ATT_KC)[None, :] - np.arange(ATT_QC)[:, None] - ATT_HALF
    band = np.abs(off) <= ATT_HALF
    tabs = []
    for dil in DILATIONS:
        b = jnp.transpose(rel_bias[_rel_bucket(off * dil)], (2, 0, 1)).astype(F32)
        tabs.append(jnp.where(jnp.asarray(band)[None], b, NEG_INF))
    return jnp.stack(tabs)


def _att_kernel(q_ref, kp_ref, kc_ref, kn_ref, vp_ref, vc_ref, vn_ref, bias_ref, o_ref,
                kwin, vwin, acc_s, m_s, l_s, *, groups):
    g = pl.program_id(0)
    first, _, slen = _seq_of_block(g, ATT_BLOCK, groups)
    pos0 = (g - first) * ATT_BLOCK

    kwin[0:ATT_PAD, :] = kp_ref[ATT_BLOCK - ATT_PAD:, :]
    kwin[ATT_PAD:ATT_PAD + ATT_BLOCK, :] = kc_ref[...]
    kwin[ATT_PAD + ATT_BLOCK:, :] = kn_ref[0:ATT_PAD, :]
    vwin[0:ATT_PAD, :] = vp_ref[ATT_BLOCK - ATT_PAD:, :]
    vwin[ATT_PAD:ATT_PAD + ATT_BLOCK, :] = vc_ref[...]
    vwin[ATT_PAD + ATT_BLOCK:, :] = vn_ref[0:ATT_PAD, :]

    scale = ATT_HEAD_DIM ** -0.5
    col = lax.broadcasted_iota(jnp.int32, (1, ATT_KC), 1)
    n_chunks = ATT_BLOCK // ATT_QC
    for bi, dil in enumerate(DILATIONS):
        bias = bias_ref[bi, 0]
        for c in range(n_chunks):
            grp, ph = c // dil, c % dil
            qs = ATT_QC * dil * grp + ph
            ks = ATT_PAD + qs - ATT_HALF * dil
            rows_q = pl.ds(qs, ATT_QC, stride=dil) if dil > 1 else pl.ds(qs, ATT_QC)
            rows_k = pl.ds(ks, ATT_KC, stride=dil) if dil > 1 else pl.ds(ks, ATT_KC)
            qc = (q_ref[rows_q, :] * scale).astype(BF16)
            kc = kwin[rows_k, :].astype(BF16)
            vc = vwin[rows_k, :].astype(BF16)
            s = lax.dot_general(qc, kc, (((1,), (1,)), ((), ())), preferred_element_type=F32)
            s = s + bias
            kpos = pos0 + (ks - ATT_PAD) + dil * col
            s = jnp.where((kpos >= 0) & (kpos < slen), s, NEG_INF)
            m = jnp.max(s, axis=-1, keepdims=True)
            p = jnp.exp(s - m)
            l = jnp.sum(p, axis=-1, keepdims=True)
            o = jnp.dot(p.astype(BF16), vc, preferred_element_type=F32)
            acc_s[bi, rows_q, :] = o
            m_s[bi, rows_q, :] = jnp.broadcast_to(m, (ATT_QC, LANES))
            l_s[bi, rows_q, :] = jnp.broadcast_to(l, (ATT_QC, LANES))

    m = jnp.maximum(jnp.maximum(m_s[0], m_s[1]), m_s[2])
    num = jnp.zeros((ATT_BLOCK, LANES), F32)
    den = jnp.zeros((ATT_BLOCK, LANES), F32)
    for bi in range(len(DILATIONS)):
        w = jnp.exp(m_s[bi] - m)
        num = num + w * acc_s[bi]
        den = den + w * l_s[bi]
    o_ref[...] = num / den


def _dilated_attention(proj, bias_tab, groups):
    t = proj.shape[0]
    nblk = t // ATT_BLOCK
    blk = (ATT_BLOCK, LANES)

    def prev_map(col):
        def f(g, h):
            first, _, _ = _seq_of_block(g, ATT_BLOCK, groups)
            return (jnp.maximum(g - 1, first), col + h)
        return f

    def next_map(col):
        def f(g, h):
            _, last, _ = _seq_of_block(g, ATT_BLOCK, groups)
            return (jnp.minimum(g + 1, last), col + h)
        return f

    def cur_map(col):
        return lambda g, h: (g, col + h)

    n_br = len(DILATIONS)
    return pl.pallas_call(
        functools.partial(_att_kernel, groups=groups),
        grid=(nblk, ATT_HEADS),
        in_specs=[pl.BlockSpec(blk, cur_map(COL_AQ)),
                  pl.BlockSpec(blk, prev_map(COL_AK)), pl.BlockSpec(blk, cur_map(COL_AK)),
                  pl.BlockSpec(blk, next_map(COL_AK)),
                  pl.BlockSpec(blk, prev_map(COL_AV)), pl.BlockSpec(blk, cur_map(COL_AV)),
                  pl.BlockSpec(blk, next_map(COL_AV)),
                  pl.BlockSpec((n_br, 1, ATT_QC, ATT_KC), lambda g, h: (0, h, 0, 0))],
        out_specs=pl.BlockSpec(blk, lambda g, h: (g, h)),
        out_shape=jax.ShapeDtypeStruct((t, ATT_WIDTH), F32),
        scratch_shapes=[pltpu.VMEM((ATT_BLOCK + 2 * ATT_PAD, LANES), F32),
                        pltpu.VMEM((ATT_BLOCK + 2 * ATT_PAD, LANES), F32),
                        pltpu.VMEM((n_br, ATT_BLOCK, LANES), F32),
                        pltpu.VMEM((n_br, ATT_BLOCK, LANES), F32),
                        pltpu.VMEM((n_br, ATT_BLOCK, LANES), F32)],
        compiler_params=_cparams(("parallel", "parallel")),
        name="dilated_attention",
    )(proj, proj, proj, proj, proj, proj, proj, bias_tab)


HG_LEVELS = (32, 16, 8, 4, 2, 1)
HG_XROWS = HG_CHUNK * (2 + len(HG_LEVELS))


def _hg_constants(backward):
    c = HG_CHUNK
    t = np.arange(c)
    if backward:
        t = c - 1 - t
    tt, uu = t[:, None], t[None, :]
    mats = [(uu <= tt), (uu > tt)]
    masks, roles = [], []
    for h in HG_LEVELS:
        same = (tt // h) == (uu // h)
        upper = ((t // h) % 2 == 1)
        seg = np.where(upper[:, None], same & (uu <= tt), same & (uu > tt))
        mats.append(seg)
        pair = ((tt // (2 * h)) == (uu // (2 * h))) & upper[:, None] & (~upper)[None, :]
        masks.append(pair)
        roles.append(upper[:, None])
    masks.append(tt == uu)
    m = np.concatenate(mats, axis=0).astype(np.float32)
    m3 = np.concatenate([m, m, m], axis=1)
    return (jnp.asarray(m3, BF16), jnp.asarray(np.stack(masks), F32),
            jnp.asarray(np.stack(roles), F32))


def _hg_block(z_ref, q_ref, v_ref, par, seg_ref, mask_ref, role_ref, st_ref, o_ref, dec_s, q_s, k_s,
              backward):
    c = HG_CHUNK
    n = HG_BLOCK // c
    log_lb, log1m_lb, om_lb = par
    z = z_ref[...]
    e = jnp.exp(-jnp.abs(z))
    log_sig = jnp.minimum(z, 0.0) - jnp.log1p(e)
    cc = log1m_lb + log_sig
    delta = jnp.minimum(jnp.abs(log_lb - cc), -NEG_INF)
    gl = jnp.maximum(log_lb, cc) + jnp.log1p(jnp.exp(-delta))
    k = om_lb * (jnp.where(z >= 0, e, 1.0) / (1.0 + e))

    g1 = gl.astype(BF16)
    r1 = gl - g1.astype(F32)
    g2 = r1.astype(BF16)
    g3 = (r1 - g2.astype(F32)).astype(BF16)
    cols = [jnp.concatenate([g[i * c:(i + 1) * c] for g in (g1, g2, g3)], axis=0) for i in range(n)]
    sums = jnp.dot(seg_ref[...], jnp.concatenate(cols, axis=1),
                   preferred_element_type=F32)
    dec = jnp.exp(sums)
    for i in range(n):
        dec_s[i] = dec[:, i * LANES:(i + 1) * LANES]
    qr = q_ref[...]
    q_s[...] = qr * jax.nn.sigmoid(qr)
    k_s[...] = k

    nt = (((1,), (1,)), ((), ()))
    last_row = 0 if backward else c - 1
    st = st_ref[...]
    for i in (range(n - 1, -1, -1) if backward else range(n)):
        rows = pl.ds(i * c, c)
        q = q_s[rows, :]
        k = k_s[rows, :]
        att = lax.dot_general(q.astype(BF16), k.astype(BF16), nt,
                              preferred_element_type=F32) * mask_ref[len(HG_LEVELS)]
        for li in range(len(HG_LEVELS)):
            y = (jnp.where(role_ref[li] > 0, q, k) * dec_s[i, (2 + li) * c:(3 + li) * c, :]).astype(BF16)
            att = att + lax.dot_general(y, y, nt, preferred_element_type=F32) * mask_ref[li]
        vb = v_ref[rows, :].astype(BF16)
        o = lax.dot_general((q * dec_s[i, 0:c, :]).astype(BF16), st.astype(BF16), nt,
                            preferred_element_type=F32)
        o_ref[rows, :] = o + jnp.dot(att.astype(BF16), vb, preferred_element_type=F32)
        kd = (k * dec_s[i, c:2 * c, :]).astype(BF16)
        upd = lax.dot_general(vb, kd, (((0,), (0,)), ((), ())), preferred_element_type=F32)
        st = st * dec_s[i, last_row:last_row + 1, :] + upd
    st_ref[...] = st


def _hgrn_kernel(qf_ref, zf_ref, vf_ref, qb_ref, zb_ref, vb_ref, par_ref,
                 segf_ref, maskf_ref, rolef_ref, segb_ref, maskb_ref, roleb_ref,
                 of_ref, ob_ref, stf_ref, stb_ref, decf_s, decb_s, qf_s, kf_s, qb_s, kb_s,
                 *, groups, nblk):
    i = pl.program_id(1)
    j = nblk - 1 - i
    first_f, _, _ = _seq_of_block(i, HG_BLOCK, groups)
    _, last_b, _ = _seq_of_block(j, HG_BLOCK, groups)

    @pl.when(i == first_f)
    def _():
        stf_ref[...] = jnp.zeros_like(stf_ref)

    @pl.when(j == last_b)
    def _():
        stb_ref[...] = jnp.zeros_like(stb_ref)

    par = par_ref[0]
    par_f = (par[0:1], par[1:2], par[2:3])
    par_b = (par[3:4], par[4:5], par[5:6])
    _hg_block(zf_ref, qf_ref, vf_ref, par_f, segf_ref, maskf_ref, rolef_ref, stf_ref, of_ref,
              decf_s, qf_s, kf_s, backward=False)
    _hg_block(zb_ref, qb_ref, vb_ref, par_b, segb_ref, maskb_ref, roleb_ref, stb_ref, ob_ref,
              decb_s, qb_s, kb_s, backward=True)


def _hgrn_scan(proj, gate_par, groups):
    t = proj.shape[0]
    nblk = t // HG_BLOCK
    blk = (HG_BLOCK, LANES)
    consts_f = _hg_constants(False)
    consts_b = _hg_constants(True)

    def fmap(col):
        return lambda h, i: (i, col + h)

    def bmap(col):
        return lambda h, i: (nblk - 1 - i, col + h)

    def const_spec(a):
        return pl.BlockSpec(a.shape, lambda h, i, nd=a.ndim: (0,) * nd)

    out = jax.ShapeDtypeStruct((t, HG_WIDTH), F32)
    return pl.pallas_call(
        functools.partial(_hgrn_kernel, groups=groups, nblk=nblk),
        grid=(HG_HEADS, nblk),
        in_specs=[pl.BlockSpec(blk, fmap(COL_HQ)), pl.BlockSpec(blk, fmap(COL_HFF)),
                  pl.BlockSpec(blk, fmap(COL_HI)),
                  pl.BlockSpec(blk, bmap(COL_HQ)), pl.BlockSpec(blk, bmap(COL_HFB)),
                  pl.BlockSpec(blk, bmap(COL_HI)),
                  pl.BlockSpec((1, 8, LANES), lambda h, i: (h, 0, 0))]
                 + [const_spec(a) for a in consts_f] + [const_spec(a) for a in consts_b],
        out_specs=[pl.BlockSpec(blk, lambda h, i: (i, h)),
                   pl.BlockSpec(blk, lambda h, i: (nblk - 1 - i, h))],
        out_shape=[out, out],
        scratch_shapes=[pltpu.VMEM((HG_DIM, HG_DIM), F32), pltpu.VMEM((HG_DIM, HG_DIM), F32),
                        pltpu.VMEM((HG_BLOCK // HG_CHUNK, HG_XROWS, LANES), F32),
                        pltpu.VMEM((HG_BLOCK // HG_CHUNK, HG_XROWS, LANES), F32)]
                       + [pltpu.VMEM(blk, F32)] * 4,
        compiler_params=_cparams(("parallel", "arbitrary")),
        name="hgrn2_scan",
    )(proj, proj, proj, proj, proj, proj, gate_par, *consts_f, *consts_b)


def _gate_params(lb_f, lb_b):
    rows = []
    for lb in (lb_f, lb_b):
        lb = lb.reshape(HG_HEADS, 1, HG_DIM)
        rows += [jnp.log(lb), jnp.log1p(-lb), 1.0 - lb]
    rows += [jnp.zeros_like(rows[0])] * 2
    return jnp.concatenate(rows, axis=1)


def _layer_lower_bounds(raw):
    p = jax.nn.softmax(raw.astype(F32), axis=0)
    c = jnp.cumsum(p, axis=0)
    return c - c[:1]


def _layernorm_rows(y, g, b):
    mu = jnp.mean(y, axis=-1, keepdims=True)
    d = y - mu
    var = jnp.mean(d * d, axis=-1, keepdims=True)
    return d * lax.rsqrt(var + LN_EPS) * g + b


def _mix_out_kernel(att_ref, of_ref, ob_ref, hg_ref, x_ref, w_ref, nw_ref, g_ref, b_ref, o_ref):
    o = of_ref[...] + ob_ref[...]
    nw = nw_ref[...]
    parts = []
    for h in range(HG_HEADS):
        oh = o[:, h * HG_DIM:(h + 1) * HG_DIM]
        parts.append(oh * lax.rsqrt(jnp.mean(oh * oh, axis=-1, keepdims=True) + RMS_EPS) * nw)
    hg = hg_ref[...]
    rec = jnp.concatenate(parts, axis=-1) * (hg * jax.nn.sigmoid(hg))
    h = jnp.dot(att_ref[...].astype(BF16), w_ref[0:ATT_WIDTH, :], preferred_element_type=F32)
    h = h + jnp.dot(rec.astype(BF16), w_ref[ATT_WIDTH:, :], preferred_element_type=F32)
    o_ref[...] = _layernorm_rows(ALPHA * x_ref[...] + h, g_ref[...], b_ref[...])


def _mix_out(att, o_f, o_b, proj, x, w_out, norm_w, ln_g, ln_b, tm):
    t = x.shape[0]
    half = pl.BlockSpec((tm, HG_WIDTH), lambda i: (i, 0))
    full = pl.BlockSpec((tm, D_MODEL), lambda i: (i, 0))
    vec = pl.BlockSpec((1, D_MODEL), lambda i: (0, 0))
    return pl.pallas_call(
        _mix_out_kernel,
        grid=(t // tm,),
        in_specs=[half, half, half,
                  pl.BlockSpec((tm, HG_WIDTH), lambda i: (i, COL_HG * LANES // HG_WIDTH)),
                  full,
                  pl.BlockSpec((D_MODEL, D_MODEL), lambda i: (0, 0), pipeline_mode=pl.Buffered(1)),
                  pl.BlockSpec((1, HG_DIM), lambda i: (0, 0)), vec, vec],
        out_specs=full,
        out_shape=jax.ShapeDtypeStruct((t, D_MODEL), F32),
        compiler_params=_cparams(("parallel",)),
        name="mixer_out_ln",
    )(att, o_f, o_b, proj, x, w_out, norm_w, ln_g, ln_b)


def _load_token_tiles(ref, rows):
    return jnp.concatenate([ref[pl.ds(j, rows, stride=TOK_TILES), :] for j in range(TOK_TILES)],
                           axis=1)


def _store_token_tiles(ref, val, rows):
    for j in range(TOK_TILES):
        ref[pl.ds(j, rows, stride=TOK_TILES), :] = val[:, j * LANES:(j + 1) * LANES]


def _mem_attn_kernel(x_ref, wq_ref, kv_ref, wo_ref, g_ref, b_ref, o_ref, ot_ref):
    x = x_ref[...]
    q = jnp.dot(x.astype(BF16), wq_ref[...], preferred_element_type=F32)
    scale = MEM_HEAD_DIM ** -0.5
    h_out = jnp.zeros(x.shape, F32)
    for h in range(MEM_HEADS):
        lo, hi = h * MEM_HEAD_DIM, (h + 1) * MEM_HEAD_DIM
        kh = kv_ref[:, lo:hi]
        vh = kv_ref[:, D_MODEL + lo:D_MODEL + hi]
        s = lax.dot_general(q[:, lo:hi].astype(BF16), kh, (((1,), (1,)), ((), ())),
                            preferred_element_type=F32) * scale
        e = jnp.exp(s - jnp.max(s, axis=-1, keepdims=True))
        p = e / jnp.sum(e, axis=-1, keepdims=True)
        oh = jnp.dot(p.astype(BF16), vh, preferred_element_type=F32)
        h_out = h_out + jnp.dot(oh.astype(BF16), wo_ref[lo:hi, :], preferred_element_type=F32)
    y = _layernorm_rows(ALPHA * x + h_out, g_ref[...], b_ref[...])
    o_ref[...] = y
    _store_token_tiles(ot_ref, y, x.shape[0])


def _mem_attention(x, kv, wq, wo, ln_g, ln_b, groups, tm):
    t = x.shape[0]
    (n0, s0), (n1, s1) = groups

    def kv_map(i):
        row = i * tm
        t0 = n0 * s0
        return (jnp.where(row < t0, row // s0, n0 + (row - t0) // s1), 0)

    full = pl.BlockSpec((tm, D_MODEL), lambda i: (i, 0))
    vec = pl.BlockSpec((1, D_MODEL), lambda i: (0, 0))
    wspec = pl.BlockSpec((D_MODEL, D_MODEL), lambda i: (0, 0), pipeline_mode=pl.Buffered(1))
    return pl.pallas_call(
        _mem_attn_kernel,
        grid=(t // tm,),
        in_specs=[full, wspec, pl.BlockSpec((MEM_TOKENS, 2 * D_MODEL), kv_map), wspec, vec, vec],
        out_specs=[full, pl.BlockSpec((tm * TOK_TILES, LANES), lambda i: (i, 0))],
        out_shape=[jax.ShapeDtypeStruct((t, D_MODEL), F32),
                   jax.ShapeDtypeStruct((t * TOK_TILES, LANES), F32)],
        compiler_params=_cparams(("parallel",)),
        name="memory_attention_ln",
    )(x, wq, kv, wo, ln_g, ln_b)


def _router_kernel(x_ref, whi_ref, wlo_ref, bias_ref, idx_ref, gw_ref):
    x = x_ref[...]
    x_hi = x.astype(BF16)
    x_lo = (x - x_hi.astype(F32)).astype(BF16)
    nt = (((1,), (1,)), ((), ()))
    whi = whi_ref[...]
    logits = (lax.dot_general(whi, x_hi, nt, preferred_element_type=F32)
              + lax.dot_general(whi, x_lo, nt, preferred_element_type=F32)
              + lax.dot_general(wlo_ref[...], x_hi, nt, preferred_element_type=F32))
    tm = logits.shape[1]
    scores = 1.0 / (1.0 + jnp.exp(-logits))
    biased = scores + bias_ref[...]
    g3 = biased.reshape(N_GROUPS, EXPERTS_PER_GROUP, tm)
    sub = lax.broadcasted_iota(jnp.int32, g3.shape, 1).astype(F32)
    m1 = jnp.max(g3, axis=1, keepdims=True)
    a1 = jnp.min(jnp.where(g3 == m1, sub, float(EXPERTS_PER_GROUP)), axis=1, keepdims=True)
    m2 = jnp.max(jnp.where(sub == a1, -jnp.inf, g3), axis=1, keepdims=True)
    gscore = (m1 + m2).reshape(N_GROUPS, tm)
    gi = lax.broadcasted_iota(jnp.int32, gscore.shape, 0).astype(F32)
    gmax = jnp.max(gscore, axis=0, keepdims=True)
    gsel = jnp.min(jnp.where(gscore == gmax, gi, float(N_GROUPS)), axis=0, keepdims=True)
    ei_int = lax.broadcasted_iota(jnp.int32, biased.shape, 0)
    ei = ei_int.astype(F32)
    egrp = (ei_int // EXPERTS_PER_GROUP).astype(F32)
    masked = jnp.where(egrp == gsel, biased, -jnp.inf)
    v1 = jnp.max(masked, axis=0, keepdims=True)
    i1 = jnp.min(jnp.where(masked == v1, ei, float(N_EXPERTS)), axis=0, keepdims=True)
    masked2 = jnp.where(ei == i1, -jnp.inf, masked)
    v2 = jnp.max(masked2, axis=0, keepdims=True)
    i2 = jnp.min(jnp.where(masked2 == v2, ei, float(N_EXPERTS)), axis=0, keepdims=True)
    w1 = jnp.sum(jnp.where(ei == i1, scores, 0.0), axis=0, keepdims=True)
    w2 = jnp.sum(jnp.where(ei == i2, scores, 0.0), axis=0, keepdims=True)
    tot = w1 + w2
    idx_ref[...] = jnp.concatenate([i1, i2], axis=0).astype(jnp.int32)
    gw_ref[...] = jnp.concatenate([w1 / tot, w2 / tot], axis=0)


def _route(x, w_hi, w_lo, bias_col, tm):
    t = x.shape[0]
    wspec = pl.BlockSpec((N_EXPERTS, D_MODEL), lambda i: (0, 0))
    return pl.pallas_call(
        _router_kernel,
        grid=(t // tm,),
        in_specs=[pl.BlockSpec((tm, D_MODEL), lambda i: (i, 0)), wspec, wspec,
                  pl.BlockSpec((N_EXPERTS, 1), lambda i: (0, 0))],
        out_specs=[pl.BlockSpec((2, tm), lambda i: (0, i)), pl.BlockSpec((2, tm), lambda i: (0, i))],
        out_shape=[jax.ShapeDtypeStruct((2, t), jnp.int32), jax.ShapeDtypeStruct((2, t), F32)],
        compiler_params=_cparams(("parallel",)),
        name="moe_router",
    )(x, w_hi, w_lo, bias_col)


def _token_copy(src3_ref, idx_ref, dst_ref, sem, r):
    slot = pl.ds(pl.multiple_of(r * TOK_TILES, TOK_TILES), TOK_TILES)
    return pltpu.make_async_copy(src3_ref.at[idx_ref[0, r]], dst_ref.at[slot, :], sem)


def _start_tokens(copies, rows):
    def issue(r, carry):
        for c in copies:
            c(r).start()
        return carry

    lax.fori_loop(0, rows, issue, 0, unroll=8)


def _wait_tokens(copies, rows):
    def drain(r, carry):
        for c in copies:
            c(r).wait()
        return carry

    lax.fori_loop(0, rows, drain, 0, unroll=8)


def _expert_gate_kernel(be_ref, nu_ref, idx_ref, idx_next_ref, xtok_ref, w_ref, o_ref, xg_ref,
                        wbf, xbuf, sem):
    b = pl.program_id(0)
    cur = [functools.partial(_token_copy, xtok_ref, idx_ref, xbuf, sem)]
    nxt = [functools.partial(_token_copy, xtok_ref, idx_next_ref, xbuf, sem)]

    @pl.when(b == 0)
    def _():
        _start_tokens(cur, MOE_BLOCK)

    _wait_tokens(cur, MOE_BLOCK)
    xg_ref[...] = _load_token_tiles(xbuf, MOE_BLOCK).astype(BF16)

    @pl.when(b + 1 < pl.num_programs(0))
    def _():
        _start_tokens(nxt, MOE_BLOCK)

    prev = be_ref[jnp.maximum(b - 1, 0)]

    @pl.when((b == 0) | (be_ref[b] != prev))
    def _():
        wbf[...] = w_ref[...].astype(BF16)

    @pl.when(b < nu_ref[0])
    def _():
        o_ref[...] = jnp.dot(xg_ref[...], wbf[...], preferred_element_type=F32).astype(o_ref.dtype)

    @pl.when(b >= nu_ref[0])
    def _():
        o_ref[...] = jnp.zeros_like(o_ref)


def _expert_gate(x_tok, row_tok, w, layer, block_expert, n_used):
    _, _, k, n = w.shape
    n_rows = row_tok.shape[0]
    nb = n_rows // MOE_BLOCK
    idx3 = row_tok.reshape(nb, 1, MOE_BLOCK)
    x3 = x_tok.reshape(x_tok.shape[0] // TOK_TILES, TOK_TILES, LANES)
    idx_block = (None, 1, MOE_BLOCK)
    return pl.pallas_call(
        _expert_gate_kernel,
        grid_spec=pltpu.PrefetchScalarGridSpec(
            num_scalar_prefetch=2, grid=(nb,),
            in_specs=[pl.BlockSpec(idx_block, lambda b, be, nu: (b, 0, 0), memory_space=pltpu.SMEM),
                      pl.BlockSpec(idx_block, lambda b, be, nu: (jnp.minimum(b + 1, nb - 1), 0, 0),
                                   memory_space=pltpu.SMEM),
                      pl.BlockSpec(memory_space=pl.ANY),
                      pl.BlockSpec((None, None, k, n), lambda b, be, nu: (layer, be[b], 0, 0))],
            out_specs=[pl.BlockSpec((MOE_BLOCK, n), lambda b, be, nu: (b, 0)),
                       pl.BlockSpec((MOE_BLOCK, k), lambda b, be, nu: (b, 0))],
            scratch_shapes=[pltpu.VMEM((k, n), BF16),
                            pltpu.VMEM((MOE_BLOCK * TOK_TILES, LANES), F32),
                            pltpu.SemaphoreType.DMA(())]),
        out_shape=[jax.ShapeDtypeStruct((n_rows, n), BF16), jax.ShapeDtypeStruct((n_rows, k), BF16)],
        compiler_params=_cparams(("arbitrary",)),
        name="expert_gather_gate",
    )(block_expert, n_used, idx3, idx3, x3, w)


def _expert_kernel(be_ref, nu_ref, *refs, mode):
    if mode == "up":
        x_ref, w_ref, gate_ref, o_ref, wbf = refs
    else:
        x_ref, w_ref, o_ref, wbf = refs
    b = pl.program_id(0)
    prev = be_ref[jnp.maximum(b - 1, 0)]

    @pl.when((b == 0) | (be_ref[b] != prev))
    def _():
        wbf[...] = w_ref[...].astype(BF16)

    @pl.when(b < nu_ref[0])
    def _():
        y = jnp.dot(x_ref[...], wbf[...], preferred_element_type=F32)
        if mode == "up":
            gt = gate_ref[...].astype(F32)
            y = gt * jax.nn.sigmoid(gt) * y
        if mode == "down":
            _store_token_tiles(o_ref, y, MOE_BLOCK)
        else:
            o_ref[...] = y.astype(o_ref.dtype)

    @pl.when(b >= nu_ref[0])
    def _():
        o_ref[...] = jnp.zeros_like(o_ref)


def _expert_matmul(x, w, layer, block_expert, n_used, gate, mode):
    _, _, k, n = w.shape
    n_rows = x.shape[0]
    x_spec = pl.BlockSpec((MOE_BLOCK, k), lambda b, be, nu: (b, 0))
    if mode == "down":
        out_spec = pl.BlockSpec((MOE_BLOCK * TOK_TILES, LANES), lambda b, be, nu: (b, 0))
        out_shape = jax.ShapeDtypeStruct((n_rows * TOK_TILES, LANES), F32)
    else:
        out_spec = pl.BlockSpec((MOE_BLOCK, n), lambda b, be, nu: (b, 0))
        out_shape = jax.ShapeDtypeStruct((n_rows, n), BF16)
    in_specs = [x_spec, pl.BlockSpec((None, None, k, n), lambda b, be, nu: (layer, be[b], 0, 0))]
    args = [x, w]
    if mode == "up":
        in_specs.append(pl.BlockSpec((MOE_BLOCK, n), lambda b, be, nu: (b, 0)))
        args.append(gate)
    return pl.pallas_call(
        functools.partial(_expert_kernel, mode=mode),
        grid_spec=pltpu.PrefetchScalarGridSpec(
            num_scalar_prefetch=2, grid=(n_rows // MOE_BLOCK,), in_specs=in_specs,
            out_specs=out_spec, scratch_shapes=[pltpu.VMEM((k, n), BF16)]),
        out_shape=out_shape,
        compiler_params=_cparams(("arbitrary",)),
        name="expert_matmul_" + mode,
    )(block_expert, n_used, *args)


def _combine_ln_kernel(p0_ref, p1_ref, p0n_ref, p1n_ref, y_ref, x_ref, gw_ref, g_ref, b_ref, o_ref,
                       buf0, buf1, sem):
    i = pl.program_id(0)
    tm = x_ref.shape[0]
    cur = [functools.partial(_token_copy, y_ref, p0_ref, buf0, sem),
           functools.partial(_token_copy, y_ref, p1_ref, buf1, sem)]
    nxt = [functools.partial(_token_copy, y_ref, p0n_ref, buf0, sem),
           functools.partial(_token_copy, y_ref, p1n_ref, buf1, sem)]

    @pl.when(i == 0)
    def _():
        _start_tokens(cur, tm)

    _wait_tokens(cur, tm)
    gw = gw_ref[...]
    o_ref[...] = (_load_token_tiles(buf0, tm) * gw[:, 0:1]) + (_load_token_tiles(buf1, tm) * gw[:, 1:2])

    @pl.when(i + 1 < pl.num_programs(0))
    def _():
        _start_tokens(nxt, tm)

    o_ref[...] = _layernorm_rows(ALPHA * x_ref[...] + o_ref[...], g_ref[...], b_ref[...])


def _combine_ln(x, y_rows, pos, gw_t, ln_g, ln_b, tm):
    t = x.shape[0]
    nb = t // tm
    full = pl.BlockSpec((tm, D_MODEL), lambda i: (i, 0))
    vec = pl.BlockSpec((1, D_MODEL), lambda i: (0, 0))
    pos4 = pos.reshape(2, nb, 1, tm)
    y3 = y_rows.reshape(y_rows.shape[0] // TOK_TILES, TOK_TILES, LANES)
    buf = pltpu.VMEM((tm * TOK_TILES, LANES), F32)
    return pl.pallas_call(
        _combine_ln_kernel,
        grid=(nb,),
        in_specs=[pl.BlockSpec((None, None, 1, tm), lambda i: (0, i, 0, 0), memory_space=pltpu.SMEM),
                  pl.BlockSpec((None, None, 1, tm), lambda i: (1, i, 0, 0), memory_space=pltpu.SMEM),
                  pl.BlockSpec((None, None, 1, tm), lambda i: (0, jnp.minimum(i + 1, nb - 1), 0, 0),
                               memory_space=pltpu.SMEM),
                  pl.BlockSpec((None, None, 1, tm), lambda i: (1, jnp.minimum(i + 1, nb - 1), 0, 0),
                               memory_space=pltpu.SMEM),
                  pl.BlockSpec(memory_space=pl.ANY),
                  full, pl.BlockSpec((tm, 2), lambda i: (i, 0)), vec, vec],
        out_specs=full,
        out_shape=jax.ShapeDtypeStruct((t, D_MODEL), F32),
        scratch_shapes=[buf, buf, pltpu.SemaphoreType.DMA(())],
        compiler_params=_cparams(("arbitrary",)),
        name="moe_combine_ln",
    )(pos4, pos4, pos4, pos4, y3, x, gw_t, ln_g, ln_b)


def _dispatch_plan(idx, t):
    n_assign = 2 * t
    n_blocks = n_assign // MOE_BLOCK + N_EXPERTS
    i32 = jnp.int32
    flat_e = idx.reshape(-1)
    ar = jnp.arange(n_assign, dtype=i32)
    _, order = lax.sort_key_val(flat_e, ar)
    _, inv = lax.sort_key_val(order, ar)
    onehot = flat_e[:, None] == jnp.arange(N_EXPERTS, dtype=i32)[None, :]
    counts = jnp.sum(onehot, axis=0, dtype=i32)
    padded = (counts + MOE_BLOCK - 1) // MOE_BLOCK * MOE_BLOCK
    pad_end = jnp.cumsum(padded)
    pad_start = pad_end - padded
    off = pad_start - (jnp.cumsum(counts) - counts)
    pos = inv + jnp.sum(jnp.where(onehot, off[None, :], 0), axis=1, dtype=i32)
    blk_start = jnp.arange(n_blocks, dtype=i32) * MOE_BLOCK
    block_expert = jnp.minimum(jnp.sum(pad_end[None, :] <= blk_start[:, None], axis=1, dtype=i32),
                               N_EXPERTS - 1)
    n_used = (pad_end[-1:] // MOE_BLOCK).astype(i32)
    rep = lambda a: jnp.repeat(a[block_expert], MOE_BLOCK)
    r = jnp.arange(n_blocks * MOE_BLOCK, dtype=i32)
    valid = (r - rep(pad_start)) < rep(counts)
    src = jnp.clip(r - rep(off), 0, n_assign - 1)
    row_tok = jnp.where(valid, order[src] % t, 0).astype(i32)
    return row_tok, pos.astype(i32), block_expert.astype(i32), n_used


def _moe(x, x_tok, layer, router, w_gate, w_up, w_down, ln_g, ln_b):
    t = x.shape[0]
    idx, gw = _route(x, *router, tm=512)
    row_tok, pos, block_expert, n_used = _dispatch_plan(idx, t)
    hg, xg = _expert_gate(x_tok, row_tok, w_gate, layer, block_expert, n_used)
    hh = _expert_matmul(xg, w_up, layer, block_expert, n_used, hg, "up")
    y_rows = _expert_matmul(hh, w_down, layer, block_expert, n_used, None, "down")
    return _combine_ln(x, y_rows, pos, gw.T, ln_g, ln_b, tm=512)


def _trunk(x, mem, groups, w_in, w_out, lb_fwd, lb_bwd, hg_norm_w, rel_bias, wq_c, wk_c, wv_c, wo_c,
           router_w, router_bias, w_gate, w_up, w_down, ln1_g, ln1_b, ln2_g, ln2_b, ln3_g, ln3_b):
    lbf = _layer_lower_bounds(lb_fwd)
    lbb = _layer_lower_bounds(lb_bwd)
    bias_tab = _att_bias_table(rel_bias)
    rw_t = router_w.T.astype(F32)
    rw_hi = rw_t.astype(BF16)
    rw_lo = (rw_t - rw_hi.astype(F32)).astype(BF16)
    router = (rw_hi, rw_lo, router_bias.astype(F32)[:, None])
    row = lambda a: a.reshape(1, -1).astype(F32)
    for l in range(DEPTH):
        proj = _matmul(x, w_in[l].astype(BF16), 1024, 1024, F32)
        att = _dilated_attention(proj, bias_tab, groups)
        o_f, o_b = _hgrn_scan(proj, _gate_params(lbf[l], lbb[l]), groups)
        x = _mix_out(att, o_f, o_b, proj, x, w_out[l].astype(BF16), row(hg_norm_w[l]),
                     row(ln1_g[l]), row(ln1_b[l]), tm=256)
        w_kv = jnp.concatenate([wk_c[l], wv_c[l]], axis=1).astype(BF16)
        kv = _matmul(mem, w_kv, 512, 1024, BF16)
        x, x_tok = _mem_attention(x, kv, wq_c[l].astype(BF16), wo_c[l].astype(BF16),
                                  row(ln2_g[l]), row(ln2_b[l]), groups, tm=256)
        x = _moe(x, x_tok, l, router, w_gate, w_up, w_down, row(ln3_g[l]), row(ln3_b[l]))
    return x


def kernel(x_prompt, x_sample, mem_prompt, mem_sample, w_in, w_out, lb_fwd, lb_bwd, hg_norm_w, rel_bias,
           wq_c, wk_c, wv_c, wo_c, router_w, router_bias, w_gate, w_up, w_down,
           ln1_g, ln1_b, ln2_g, ln2_b, ln3_g, ln3_b):
    n0, s0, d = x_prompt.shape
    n1, s1, _ = x_sample.shape
    groups = ((n0, s0), (n1, s1))
    x = jnp.concatenate([x_prompt.reshape(n0 * s0, d), x_sample.reshape(n1 * s1, d)], axis=0)
    mem = jnp.concatenate([mem_prompt.reshape(-1, d), mem_sample.reshape(-1, d)], axis=0)
    y = _trunk(x, mem, groups, w_in, w_out, lb_fwd, lb_bwd, hg_norm_w, rel_bias, wq_c, wk_c, wv_c, wo_c,
               router_w, router_bias, w_gate, w_up, w_down, ln1_g, ln1_b, ln2_g, ln2_b, ln3_g, ln3_b)
    return (y[:n0 * s0].reshape(n0, s0, d), y[n0 * s0:].reshape(n1, s1, d))
```
